```python
import jax
import jax.numpy as jnp
from jax import lax
import numpy as np

D_MODEL = 1024
BATCH = 32
SEQ = 2048
DEPTH = 2

BLOCK = 128
RET_HEADS = 4
RET_DIM = 128
RET_WIDTH = RET_HEADS * RET_DIM
SG_GROUPS = 4
SG_DIM = 128
SG_WIDTH = SG_GROUPS * SG_DIM
EVEN_IN = 4 * RET_WIDTH + 2 * SG_WIDTH
EVEN_OUT = RET_WIDTH + SG_WIDTH
SB_HEADS = 16
SB_DIM = 64
SB_WIDTH = SB_HEADS * SB_DIM
ODD_IN = 3 * SB_WIDTH
MOE_GROUPS = 4
MOE_PER_GROUP = 8
MOE_EXPERTS = MOE_GROUPS * MOE_PER_GROUP
MOE_TOP_K = 2
MOE_HIDDEN = 256
PLE_DIM = 256
ROPE_BASE = 10000.0
EPS = 1e-6

kernel_name = 'hybrid_retnet_sgmlp_stickbreak_hmoe'


def rmsnorm(x, w):
    xf = x.astype(jnp.float32)
    y = xf * lax.rsqrt(jnp.mean(xf * xf, axis=-1, keepdims=True) + EPS)
    return (y * w.astype(jnp.float32)).astype(x.dtype)


def rotary(x):
    s, d = x.shape[1], x.shape[-1]
    inv_freq = ROPE_BASE ** (-jnp.arange(0, d, 2, dtype=jnp.float32) / d)
    ang = jnp.arange(s, dtype=jnp.float32)[:, None] * inv_freq[None, :]
    cos = jnp.cos(ang)[None, :, None, :]
    sin = jnp.sin(ang)[None, :, None, :]
    xf = x.astype(jnp.float32)
    x1, x2 = xf[..., : d // 2], xf[..., d // 2:]
    return jnp.concatenate([x1 * cos - x2 * sin, x1 * sin + x2 * cos], axis=-1)


def retention(q, k, v):
    b, s, h, d = q.shape
    nc = s // BLOCK
    log_gamma = jnp.log1p(-jnp.exp2(-5.0 - jnp.arange(h, dtype=jnp.float32)))
    pos = jnp.arange(BLOCK, dtype=jnp.float32)
    diff = pos[:, None] - pos[None, :]
    intra = jnp.where(diff >= 0, jnp.exp(log_gamma[:, None, None] * jnp.maximum(diff, 0.0)), 0.0)
    q_dec = jnp.exp(log_gamma[:, None] * (pos[None, :] + 1.0))[None, :, :, None]
    k_dec = jnp.exp(log_gamma[:, None] * (BLOCK - 1.0 - pos[None, :]))[None, :, :, None]
    chunk_dec = jnp.exp(log_gamma * BLOCK)[None, :, None, None]

    def to_chunks(t):
        return t.astype(jnp.float32).reshape(b, nc, BLOCK, h, d).transpose(1, 0, 3, 2, 4)

    def step(state, inp):
        qi, ki, vi = inp
        scores = jnp.einsum('bhid,bhjd->bhij', qi, ki) * intra
        out = jnp.einsum('bhij,bhje->bhie', scores, vi) + jnp.einsum('bhid,bhde->bhie', qi * q_dec, state)
        state = state * chunk_dec + jnp.einsum('bhjd,bhje->bhde', ki * k_dec, vi)
        return state, out

    state0 = jnp.zeros((b, h, d, v.shape[-1]), jnp.float32)
    _, o = lax.scan(step, state0, (to_chunks(q), to_chunks(k), to_chunks(v)))
    return o.transpose(1, 0, 3, 2, 4).reshape(b, s, h, v.shape[-1])


def head_norm(o, w):
    b, s = o.shape[0], o.shape[1]
    mu = jnp.mean(o, axis=-1, keepdims=True)
    var = jnp.mean(jnp.square(o - mu), axis=-1, keepdims=True)
    return ((o - mu) * lax.rsqrt(var + EPS)).reshape(b, s, -1) * w.astype(jnp.float32)


def spatial_gating(u, v, w_s, b_s):
    b, s, _ = v.shape
    nc = s // BLOCK
    w = jnp.where(jnp.tril(jnp.ones((BLOCK, BLOCK), dtype=bool)), w_s, jnp.zeros_like(w_s))
    vb = v.reshape(b, nc, BLOCK, SG_GROUPS, SG_DIM)
    mixed = jnp.einsum('gts,bnsgc->bntgc', w, vb) + b_s.T[None, None, :, :, None]
    return u * mixed.reshape(b, s, SG_WIDTH)


def stick_breaking(q, k, v):
    b, s, h, d = q.shape
    scale = d ** -0.5
    qh = q.transpose(0, 2, 1, 3)
    kh = k.transpose(0, 2, 1, 3)
    vh = v.transpose(0, 2, 1, 3)
    outs = []
    for blk in range(s // BLOCK):
        start, end = blk * BLOCK, (blk + 1) * BLOCK
        qb = qh[:, :, start:end].astype(jnp.float32)
        kb = kh[:, :, :end].astype(jnp.float32)
        vb = vh[:, :, :end].astype(jnp.float32)
        z = jnp.einsum('bhtd,bhsd->bhts', qb, kb) * scale
        causal = jnp.arange(end)[None, :] < jnp.arange(start, end)[:, None]
        log_stay = jnp.where(causal, jax.nn.log_sigmoid(-z), 0.0)
        after = lax.cumsum(log_stay, axis=3, reverse=True) - log_stay
        a = jnp.where(causal, jnp.exp(jax.nn.log_sigmoid(z) + after), 0.0)
        outs.append(jnp.einsum('bhts,bhsd->bhtd', a, vb))
    o = jnp.concatenate(outs, axis=2)
    return o.transpose(0, 2, 1, 3).reshape(b, s, h * d).astype(q.dtype)


def even_mixer(xn, w_in, w_out, ret_norm_w, sg_norm_w, sg_w, sg_b):
    b, s, _ = xn.shape
    proj = xn @ w_in
    q, k, v, g, u, vs = jnp.split(proj, [RET_WIDTH, 2 * RET_WIDTH, 3 * RET_WIDTH, 4 * RET_WIDTH,
                                         4 * RET_WIDTH + SG_WIDTH], axis=-1)
    q = rotary(q.reshape(b, s, RET_HEADS, RET_DIM))
    k = rotary(k.reshape(b, s, RET_HEADS, RET_DIM)) * (RET_DIM ** -0.5)
    v = v.reshape(b, s, RET_HEADS, RET_DIM)
    ret = head_norm(retention(q, k, v), ret_norm_w).astype(xn.dtype)
    ret_out = jax.nn.silu(g) * ret
    vs = rmsnorm(jax.nn.gelu(vs), sg_norm_w)
    sg_out = spatial_gating(jax.nn.gelu(u), vs, sg_w, sg_b)
    return jnp.concatenate([ret_out, sg_out], axis=-1) @ w_out


def odd_mixer(xn, w_in, w_out):
    b, s, _ = xn.shape
    q, k, v = jnp.split(xn @ w_in, 3, axis=-1)
    shp = (b, s, SB_HEADS, SB_DIM)
    o = stick_breaking(q.reshape(shp), k.reshape(shp), v.reshape(shp))
    return o @ w_out


def hier_moe(x, w_group, b_group, w_expert, b_expert, w_gate, w_up, w_down):
    b, s, dm = x.shape
    xt = x.reshape(-1, dm)
    g_probs = jax.nn.softmax((xt @ w_group + b_group).astype(jnp.float32), axis=-1)
    g_w, g_idx = lax.top_k(g_probs, 1)
    e_logits = (xt @ w_expert + b_expert).astype(jnp.float32).reshape(-1, MOE_GROUPS, MOE_PER_GROUP)
    e_in = jnp.take_along_axis(e_logits, g_idx[:, :, None], axis=1)[:, 0]
    top_v, top_i = lax.top_k(e_in, MOE_TOP_K)
    top_w = jax.nn.softmax(top_v, axis=-1) * g_w
    expert_id = g_idx * MOE_PER_GROUP + top_i
    gates = jnp.sum(jax.nn.one_hot(expert_id, MOE_EXPERTS, dtype=jnp.float32) * top_w[..., None], axis=1)
    y = jnp.zeros((xt.shape[0], dm), jnp.float32)
    for e in range(MOE_EXPERTS):
        hid = jax.nn.silu(xt @ w_gate[e]) * (xt @ w_up[e])
        y = y + gates[:, e:e + 1] * (hid @ w_down[e]).astype(jnp.float32)
    return y.astype(x.dtype).reshape(b, s, dm)


def per_layer_embedding(h, p_i, w_norm, w_gate, w_proj):
    gate = jax.nn.sigmoid(rmsnorm(h, w_norm) @ w_gate)
    return h + gate * (p_i @ w_proj)


def _normal(key, shape, scale):
    return jax.random.normal(key, shape, jnp.float32) * scale


def _gain(key, shape):
    return 1.0 + 0.02 * jax.random.normal(key, shape, jnp.float32)


def setup_inputs(seed: int = 0) -> dict:
    key = jax.random.key(seed)
    ks = jax.random.split(key, 24)
    n_even = (DEPTH + 1) // 2
    n_odd = DEPTH // 2
    return {
        'x': _normal(ks[0], (BATCH, SEQ, D_MODEL), 1.0),
        'p': _normal(ks[1], (DEPTH, BATCH, SEQ, PLE_DIM), 1.0),
        'attn_norm_w': _gain(ks[2], (DEPTH, D_MODEL)),
        'ffn_norm_w': _gain(ks[3], (DEPTH, D_MODEL)),
        'final_norm_w': _gain(ks[4], (D_MODEL,)),
        'even_w_in': _normal(ks[5], (n_even, D_MODEL, EVEN_IN), D_MODEL ** -0.5),
        'even_w_out': _normal(ks[6], (n_even, EVEN_OUT, D_MODEL), EVEN_OUT ** -0.5),
        'ret_norm_w': _gain(ks[7], (n_even, RET_WIDTH)),
        'sg_norm_w': _gain(ks[8], (n_even, SG_WIDTH)),
        'sg_spatial_w': _normal(ks[9], (n_even, SG_GROUPS, BLOCK, BLOCK), 0.5 * BLOCK ** -0.5),
        'sg_spatial_b': _gain(ks[10], (n_even, SG_GROUPS, BLOCK)),
        'odd_w_in': _normal(ks[11], (n_odd, D_MODEL, ODD_IN), D_MODEL ** -0.5),
        'odd_w_out': _normal(ks[12], (n_odd, SB_WIDTH, D_MODEL), SB_WIDTH ** -0.5),
        'moe_w_group': _normal(ks[13], (DEPTH, D_MODEL, MOE_GROUPS), D_MODEL ** -0.5),
        'moe_b_group': _normal(ks[14], (DEPTH, MOE_GROUPS), 0.01),
        'moe_w_expert': _normal(ks[15], (DEPTH, D_MODEL, MOE_EXPERTS), D_MODEL ** -0.5),
        'moe_b_expert': _normal(ks[16], (DEPTH, MOE_EXPERTS), 0.01),
        'moe_w_gate': _normal(ks[17], (DEPTH, MOE_EXPERTS, D_MODEL, MOE_HIDDEN), D_MODEL ** -0.5),
        'moe_w_up': _normal(ks[18], (DEPTH, MOE_EXPERTS, D_MODEL, MOE_HIDDEN), D_MODEL ** -0.5),
        'moe_w_down': _normal(ks[19], (DEPTH, MOE_EXPERTS, MOE_HIDDEN, D_MODEL), MOE_HIDDEN ** -0.5),
        'ple_norm_w': _gain(ks[20], (DEPTH, D_MODEL)),
        'ple_w_gate': _normal(ks[21], (DEPTH, D_MODEL, D_MODEL), D_MODEL ** -0.5),
        'ple_w_proj': _normal(ks[22], (DEPTH, PLE_DIM, D_MODEL), PLE_DIM ** -0.5),
    }


def reference(x, p, attn_norm_w, ffn_norm_w, final_norm_w, even_w_in, even_w_out, ret_norm_w,
              sg_norm_w, sg_spatial_w, sg_spatial_b, odd_w_in, odd_w_out, moe_w_group, moe_b_group,
              moe_w_expert, moe_b_expert, moe_w_gate, moe_w_up, moe_w_down, ple_norm_w, ple_w_gate,
              ple_w_proj):
    h = x
    for i in range(DEPTH):
        j = i // 2
        xn = rmsnorm(h, attn_norm_w[i])
        if i % 2 == 0:
            h = h + even_mixer(xn, even_w_in[j], even_w_out[j], ret_norm_w[j], sg_norm_w[j],
                               sg_spatial_w[j], sg_spatial_b[j])
        else:
            h = h + odd_mixer(xn, odd_w_in[j], odd_w_out[j])
        h = h + hier_moe(rmsnorm(h, ffn_norm_w[i]), moe_w_group[i], moe_b_group[i], moe_w_expert[i],
                         moe_b_expert[i], moe_w_gate[i], moe_w_up[i], moe_w_down[i])
        h = per_layer_embedding(h, p[i], ple_norm_w[i], ple_w_gate[i], ple_w_proj[i])
    return rmsnorm(h, final_norm_w)
```

```python
import functools

import jax
import jax.numpy as jnp
from jax import lax
from jax.experimental import pallas as pl
from jax.experimental.pallas import tpu as pltpu

F32 = jnp.float32
BF16 = jnp.bfloat16
I32 = jnp.int32

LANES = 128
SUBLANES = 8
CHUNK = 128
RET_HEADS = 4
RET_DIM = 128
RET_WIDTH = RET_HEADS * RET_DIM
SG_GROUPS = 4
SG_DIM = 128
SG_WIDTH = SG_GROUPS * SG_DIM
SB_DIM = 64
MOE_GROUPS = 4
MOE_PER_GROUP = 8
MOE_EXPERTS = MOE_GROUPS * MOE_PER_GROUP
MOE_PAIRS = MOE_PER_GROUP * (MOE_PER_GROUP - 1) // 2
MOE_BUCKETS = MOE_GROUPS * MOE_PAIRS
ROPE_BASE = 10000.0
EPS = 1e-6

ROW_TILE = 512
MIX_TILE = 512
MOE_TILE = 256
VMEM_LIMIT = 56 * 1024 * 1024


def _params(*sem):
    return pltpu.CompilerParams(dimension_semantics=sem, vmem_limit_bytes=VMEM_LIMIT)


def _rms(x, w):
    return x * lax.rsqrt(jnp.mean(x * x, axis=-1, keepdims=True) + EPS) * w


def _dot(a, b):
    return jnp.dot(a, b, preferred_element_type=F32)


def _dot_nt(a, b):
    return lax.dot_general(a, b, (((1,), (1,)), ((), ())), preferred_element_type=F32)


def _dot_tn(a, b):
    return lax.dot_general(a, b, (((0,), (0,)), ((), ())), preferred_element_type=F32)


def _split_bf16(x):
    hi = x.astype(BF16)
    lo = (x - hi.astype(F32)).astype(BF16)
    return hi, lo


def _norm_matmul_kernel(h_ref, nw_ref, w_ref, o_ref, *, col_chunk):
    xn = _rms(h_ref[...], nw_ref[...]).astype(BF16)
    for c in range(0, o_ref.shape[1], col_chunk):
        o_ref[:, c:c + col_chunk] = _dot(xn, w_ref[:, c:c + col_chunk]).astype(o_ref.dtype)


def _norm_matmul(h, norm_w, w):
    n, d = h.shape
    m = w.shape[1]
    return pl.pallas_call(
        functools.partial(_norm_matmul_kernel, col_chunk=512),
        grid=(n // ROW_TILE,),
        in_specs=[pl.BlockSpec((ROW_TILE, d), lambda i: (i, 0)),
                  pl.BlockSpec((1, d), lambda i: (0, 0)),
                  pl.BlockSpec((d, m), lambda i: (0, 0))],
        out_specs=pl.BlockSpec((ROW_TILE, m), lambda i: (i, 0)),
        out_shape=jax.ShapeDtypeStruct((n, m), BF16),
        compiler_params=_params("parallel"),
        name="norm_matmul",
    )(h, norm_w.reshape(1, d), w)


def _matmul_residual_kernel(a_ref, w_ref, h_ref, o_ref):
    o_ref[...] = h_ref[...] + _dot(a_ref[...], w_ref[...])


def _matmul_residual(a, w, h):
    n, k = a.shape
    d = w.shape[1]
    return pl.pallas_call(
        _matmul_residual_kernel,
        grid=(n // ROW_TILE,),
        in_specs=[pl.BlockSpec((ROW_TILE, k), lambda i: (i, 0)),
                  pl.BlockSpec((k, d), lambda i: (0, 0)),
                  pl.BlockSpec((ROW_TILE, d), lambda i: (i, 0))],
        out_specs=pl.BlockSpec((ROW_TILE, d), lambda i: (i, 0)),
        out_shape=jax.ShapeDtypeStruct((n, d), F32),
        compiler_params=_params("parallel"),
        name="matmul_residual",
    )(a, w, h)


def _even_mixer_kernel(h_ref, proj_ref, cos_ref, sin_ref, intra_ref, qdec_ref, kdec_ref, cdec_ref,
                       retw_ref, sgnw_ref, sgw_ref, sgb_ref, wout_ref, o_ref, state_ref, mixed_ref):
    @pl.when(pl.program_id(1) == 0)
    def _():
        state_ref[...] = jnp.zeros_like(state_ref)

    def chunk(c, carry):
        r0 = pl.multiple_of(c * CHUNK, CHUNK)
        rows = pl.ds(r0, CHUNK)
        cos = cos_ref[rows, :]
        sin = sin_ref[rows, :]

        def rope(t):
            return t * cos + pltpu.roll(t, RET_DIM // 2, 1) * sin

        for hd in range(RET_HEADS):
            lanes = slice(hd * RET_DIM, (hd + 1) * RET_DIM)
            q = rope(proj_ref[rows, lanes].astype(F32))
            k = rope(proj_ref[rows, RET_WIDTH + hd * RET_DIM:RET_WIDTH + (hd + 1) * RET_DIM].astype(F32))
            v = proj_ref[rows, 2 * RET_WIDTH + hd * RET_DIM:2 * RET_WIDTH + (hd + 1) * RET_DIM]
            g = proj_ref[rows, 3 * RET_WIDTH + hd * RET_DIM:3 * RET_WIDTH + (hd + 1) * RET_DIM].astype(F32)
            scores = _dot_nt(q.astype(BF16), k.astype(BF16)) * intra_ref[hd]
            st = state_ref[hd]
            out = _dot(scores.astype(BF16), v) + _dot((q * qdec_ref[hd]).astype(BF16), st.astype(BF16))
            state_ref[hd] = st * cdec_ref[hd] + _dot_tn((k * kdec_ref[hd]).astype(BF16), v)
            mu = jnp.mean(out, axis=-1, keepdims=True)
            cen = out - mu
            var = jnp.mean(cen * cen, axis=-1, keepdims=True)
            ret = cen * lax.rsqrt(var + EPS) * retw_ref[:, lanes]
            mixed_ref[rows, lanes] = (jax.nn.silu(g) * ret).astype(BF16)

        base = 4 * RET_WIDTH
        gv = jax.nn.gelu(proj_ref[rows, base + SG_WIDTH:base + 2 * SG_WIDTH].astype(F32))
        vsn = _rms(gv, sgnw_ref[...]).astype(BF16)
        for gi in range(SG_GROUPS):
            lanes = slice(gi * SG_DIM, (gi + 1) * SG_DIM)
            u = proj_ref[rows, base + gi * SG_DIM:base + (gi + 1) * SG_DIM].astype(F32)
            mix = _dot(sgw_ref[gi], vsn[:, lanes]) + sgb_ref[gi]
            mixed_ref[rows, RET_WIDTH + gi * SG_DIM:RET_WIDTH + (gi + 1) * SG_DIM] = (
                jax.nn.gelu(u) * mix).astype(BF16)
        return carry

    lax.fori_loop(0, h_ref.shape[0] // CHUNK, chunk, 0)
    o_ref[...] = h_ref[...] + _dot(mixed_ref[...], wout_ref[...])


def _even_mixer(h, proj, batch, seq, w_out, ret_norm_w, sg_norm_w, sg_w, sg_b):
    n, d = h.shape
    steps = seq // MIX_TILE
    inv_freq = ROPE_BASE ** (-jnp.arange(0, RET_DIM, 2, dtype=F32) / RET_DIM)
    ang = jnp.arange(seq, dtype=F32)[:, None] * inv_freq[None, :]
    cos = jnp.concatenate([jnp.cos(ang), jnp.cos(ang)], axis=-1)
    sin = jnp.concatenate([-jnp.sin(ang), jnp.sin(ang)], axis=-1)
    log_gamma = jnp.log1p(-jnp.exp2(-5.0 - jnp.arange(RET_HEADS, dtype=F32)))
    pos = jnp.arange(CHUNK, dtype=F32)
    diff = pos[:, None] - pos[None, :]
    scale = RET_DIM ** -0.5
    intra = jnp.where(diff >= 0, jnp.exp(log_gamma[:, None, None] * jnp.maximum(diff, 0.0)), 0.0) * scale
    bshape = (RET_HEADS, CHUNK, RET_DIM)
    qdec = jnp.broadcast_to(jnp.exp(log_gamma[:, None] * (pos[None, :] + 1.0))[:, :, None], bshape)
    kdec = jnp.broadcast_to(jnp.exp(log_gamma[:, None] * (CHUNK - 1.0 - pos[None, :]))[:, :, None] * scale, bshape)
    cdec = jnp.broadcast_to(jnp.exp(log_gamma * CHUNK)[:, None, None], bshape)
    tril = jnp.tril(jnp.ones((CHUNK, CHUNK), dtype=bool))
    sgw = jnp.where(tril, sg_w, jnp.zeros_like(sg_w)).astype(BF16)
    sgb = jnp.broadcast_to(sg_b[:, :, None], (SG_GROUPS, CHUNK, SG_DIM))

    const3 = lambda b, s: (0, 0, 0)
    const2 = lambda b, s: (0, 0)
    row = lambda b, s: (b * steps + s, 0)
    return pl.pallas_call(
        _even_mixer_kernel,
        grid=(batch, steps),
        in_specs=[pl.BlockSpec((MIX_TILE, d), row),
                  pl.BlockSpec((MIX_TILE, proj.shape[1]), row),
                  pl.BlockSpec((MIX_TILE, RET_DIM), lambda b, s: (s, 0)),
                  pl.BlockSpec((MIX_TILE, RET_DIM), lambda b, s: (s, 0)),
                  pl.BlockSpec(bshape, const3),
                  pl.BlockSpec(bshape, const3),
                  pl.BlockSpec(bshape, const3),
                  pl.BlockSpec(bshape, const3),
                  pl.BlockSpec((1, RET_WIDTH), const2),
                  pl.BlockSpec((1, SG_WIDTH), const2),
                  pl.BlockSpec((SG_GROUPS, CHUNK, CHUNK), const3),
                  pl.BlockSpec((SG_GROUPS, CHUNK, SG_DIM), const3),
                  pl.BlockSpec(w_out.shape, const2)],
        out_specs=pl.BlockSpec((MIX_TILE, d), row),
        out_shape=jax.ShapeDtypeStruct((n, d), F32),
        scratch_shapes=[pltpu.VMEM((RET_HEADS, RET_DIM, RET_DIM), F32),
                        pltpu.VMEM((MIX_TILE, RET_WIDTH + SG_WIDTH), BF16)],
        compiler_params=_params("parallel", "arbitrary"),
        name="even_mixer",
    )(h, proj, cos, sin, intra, qdec, kdec, cdec, ret_norm_w.reshape(1, -1), sg_norm_w.reshape(1, -1),
      sgw, sgb, w_out)


def _stick_kernel(q_ref, k_ref, v_ref, cum_ref, o_ref):
    qi = pl.program_id(2)
    row = lax.broadcasted_iota(I32, (CHUNK, CHUNK), 0)
    lane = lax.broadcasted_iota(I32, (CHUNK, CHUNK), 1)
    causal = lane < row
    q = q_ref[...] * (SB_DIM ** -0.5)
    cum = cum_ref[...]

    def block(j, qm, carry, acc, diagonal):
        keys = pl.ds(pl.multiple_of(j * CHUNK, CHUNK), CHUNK)
        z = _dot_nt(qm, k_ref[keys, :])
        stay = jnp.minimum(-z, 0.0) - jnp.log(1.0 + jnp.exp(-jnp.abs(z)))
        if diagonal:
            stay = jnp.where(causal, stay, 0.0)
        hi, lo = _split_bf16(stay)
        sums = _dot(hi, cum) + _dot(lo, cum)
        a = jnp.exp(z + carry + sums[:, :CHUNK])
        if diagonal:
            a = jnp.where(causal, a, 0.0)
        acc = acc + _dot(a.astype(BF16), v_ref[keys, :])
        return carry + sums[:, CHUNK:], acc

    outs = []
    for half in range(2):
        in_head = (lane >= half * SB_DIM) & (lane < (half + 1) * SB_DIM)
        qm = jnp.where(in_head, q, jnp.zeros_like(q))
        zero = jnp.zeros((CHUNK, CHUNK), F32)
        carry, acc = block(qi, qm, zero, zero, True)

        def body(it, ca, qm=qm):
            return block(qi - 1 - it, qm, ca[0], ca[1], False)

        carry, acc = lax.fori_loop(0, qi, body, (carry, acc))
        outs.append(acc)
    o_ref[...] = jnp.where(lane < SB_DIM, outs[0], outs[1]).astype(o_ref.dtype)


def _stick_breaking(qkv, batch, seq):
    n = qkv.shape[0]
    width = qkv.shape[1] // 3
    pairs = width // LANES
    nq = seq // CHUNK
    qkv3 = qkv.reshape(batch, seq, 3 * width)
    tri = (jnp.arange(CHUNK)[:, None] >= jnp.arange(CHUNK)[None, :])
    cum = jnp.concatenate([tri.astype(BF16), jnp.ones((CHUNK, CHUNK), BF16)], axis=1)
    return pl.pallas_call(
        _stick_kernel,
        grid=(batch, pairs, nq),
        in_specs=[pl.BlockSpec((CHUNK, LANES), lambda b, p, i: (b * nq + i, p)),
                  pl.BlockSpec((None, seq, LANES), lambda b, p, i: (b, 0, pairs + p)),
                  pl.BlockSpec((None, seq, LANES), lambda b, p, i: (b, 0, 2 * pairs + p)),
                  pl.BlockSpec((CHUNK, 2 * CHUNK), lambda b, p, i: (0, 0))],
        out_specs=pl.BlockSpec((CHUNK, LANES), lambda b, p, i: (b * nq + i, p)),
        out_shape=jax.ShapeDtypeStruct((n, width), BF16),
        compiler_params=_params("parallel", "parallel", "arbitrary"),
        name="stick_breaking",
    )(qkv, qkv3, qkv3, cum)


def _router_logits(xn, whi_ref, wlo_ref, b_ref):
    hi, lo = _split_bf16(xn)
    return _dot(hi, whi_ref[...]) + _dot(hi, wlo_ref[...]) + _dot(lo, whi_ref[...]) + b_ref[...]


def _router_kernel(h_ref, nw_ref, whi_ref, wlo_ref, b_ref, o_ref):
    logits = _router_logits(_rms(h_ref[...], nw_ref[...]), whi_ref, wlo_ref, b_ref)
    lane = lax.broadcasted_iota(I32, logits.shape, 1)
    neg = jnp.float32(-jnp.inf)

    def first_max(vals):
        m = jnp.max(vals, axis=-1, keepdims=True)
        return jnp.min(jnp.where(vals == m, lane, LANES), axis=-1, keepdims=True)

    group = first_max(jnp.where(lane < MOE_GROUPS, logits, neg))
    rel = lane - MOE_GROUPS
    in_group = (rel >= 0) & (rel < MOE_EXPERTS) & ((rel >> 3) == group)
    ev = jnp.where(in_group, logits, neg)
    i1 = first_max(ev)
    i2 = first_max(jnp.where(lane == i1, neg, ev))
    a = (i1 - MOE_GROUPS) & (MOE_PER_GROUP - 1)
    b = (i2 - MOE_GROUPS) & (MOE_PER_GROUP - 1)
    lo = jnp.minimum(a, b)
    hi = jnp.maximum(a, b)
    pair = (lo * (2 * MOE_PER_GROUP - 1 - lo)) // 2 + (hi - lo - 1)
    o_ref[...] = group * MOE_PAIRS + pair


def _router(h, norm_w, whi, wlo, bias):
    n, d = h.shape
    const = lambda i: (0, 0)
    return pl.pallas_call(
        _router_kernel,
        grid=(n // ROW_TILE,),
        in_specs=[pl.BlockSpec((ROW_TILE, d), lambda i: (i, 0)),
                  pl.BlockSpec((1, d), const),
                  pl.BlockSpec((d, LANES), const),
                  pl.BlockSpec((d, LANES), const),
                  pl.BlockSpec((1, LANES), const)],
        out_specs=pl.BlockSpec((ROW_TILE, 1), lambda i: (i, 0)),
        out_shape=jax.ShapeDtypeStruct((n, 1), I32),
        compiler_params=_params("parallel"),
        name="moe_router",
    )(h, norm_w.reshape(1, d), whi, wlo, bias)


def _route_tables(bucket, n):
    t = MOE_TILE
    n_tiles = n // t + MOE_BUCKETS
    counts = jnp.zeros((MOE_BUCKETS,), I32).at[bucket].add(1)
    tiles = (counts + t - 1) // t
    tile_end = jnp.cumsum(tiles)
    tile_start = tile_end - tiles
    count_start = jnp.cumsum(counts) - counts
    order = jnp.argsort(bucket, stable=True).astype(I32)
    sorted_bucket = bucket[order]
    slot = tile_start[sorted_bucket] * t + jnp.arange(n, dtype=I32) - count_start[sorted_bucket]
    src = jnp.full((n_tiles * t,), n, I32).at[slot].set(order)
    tile_id = jnp.arange(n_tiles, dtype=I32)
    valid = tile_id < tile_end[-1]
    last = jnp.searchsorted(tile_end, tile_end[-1] - 1, side="right").astype(I32)
    tile_bucket = jnp.where(valid, jnp.searchsorted(tile_end, tile_id, side="right").astype(I32), last)
    rows = jnp.clip(counts[tile_bucket] - (tile_id - tile_start[tile_bucket]) * t, 0, t)
    rows = jnp.where(valid, rows, 0)
    group = tile_bucket // MOE_PAIRS
    pair = tile_bucket % MOE_PAIRS
    lo_tab, hi_tab = [], []
    for lo in range(MOE_PER_GROUP):
        for hi in range(lo + 1, MOE_PER_GROUP):
            lo_tab.append(lo)
            hi_tab.append(hi)
    e_lo = group * MOE_PER_GROUP + jnp.asarray(lo_tab, I32)[pair]
    e_hi = group * MOE_PER_GROUP + jnp.asarray(hi_tab, I32)[pair]
    return src, rows.astype(I32), group, e_lo, e_hi


def _moe_kernel(src_ref, rows_ref, group_ref, elo_ref, ehi_ref,
                h_hbm, nw_ref, whi_ref, wlo_ref, b_ref, wg1_ref, wu1_ref, wd1_ref, wg2_ref, wu2_ref, wd2_ref,
                out_hbm, xbuf, obuf, gsem, ssem, *, n_rows):
    i = pl.program_id(0)
    base = i * MOE_TILE
    real = rows_ref[i]

    @pl.when(real > 0)
    def _():
        def gather(r, c):
            tok = jnp.minimum(src_ref[base + r], n_rows - 1)
            pltpu.make_async_copy(h_hbm.at[tok], xbuf.at[r], gsem).start()
            return c

        lax.fori_loop(0, MOE_TILE, gather, 0)
        pltpu.make_async_copy(h_hbm.at[pl.ds(0, MOE_TILE)], xbuf, gsem).wait()

        x = xbuf[...]
        xn = _rms(x, nw_ref[...])
        logits = _router_logits(xn, whi_ref, wlo_ref, b_ref)
        lane = lax.broadcasted_iota(I32, logits.shape, 1)

        def pick(idx):
            return jnp.sum(jnp.where(lane == idx, logits, 0.0), axis=-1, keepdims=True)

        lg = pick(group_ref[i])
        gexp = jnp.exp(jnp.where(lane < MOE_GROUPS, logits - lg, -jnp.inf))
        group_w = 1.0 / jnp.sum(gexp, axis=-1, keepdims=True)
        l1 = pick(MOE_GROUPS + elo_ref[i])
        l2 = pick(MOE_GROUPS + ehi_ref[i])
        m = jnp.maximum(l1, l2)
        e1 = jnp.exp(l1 - m)
        e2 = jnp.exp(l2 - m)
        w1 = e1 / (e1 + e2) * group_w
        w2 = e2 / (e1 + e2) * group_w

        xb = xn.astype(BF16)
        hid1 = jax.nn.silu(_dot(xb, wg1_ref[...])) * _dot(xb, wu1_ref[...]) * w1
        hid2 = jax.nn.silu(_dot(xb, wg2_ref[...])) * _dot(xb, wu2_ref[...]) * w2
        y = _dot(hid1.astype(BF16), wd1_ref[...]) + _dot(hid2.astype(BF16), wd2_ref[...])
        obuf[...] = x + y

        def scatter(r, c):
            pltpu.make_async_copy(obuf.at[r], out_hbm.at[src_ref[base + r]], ssem).start()
            return c

        lax.fori_loop(0, real, scatter, 0)
        whole = pl.multiple_of((real // SUBLANES) * SUBLANES, SUBLANES)

        @pl.when(whole > 0)
        def _():
            pltpu.make_async_copy(obuf.at[pl.ds(0, whole)], out_hbm.at[pl.ds(0, whole)], ssem).wait()

        def drain(r, c):
            pltpu.make_async_copy(obuf.at[0], out_hbm.at[0], ssem).wait()
            return c

        lax.fori_loop(0, real - whole, drain, 0)


def _moe(h, norm_w, whi, wlo, bias, w_gate, w_up, w_down, tables):
    n, d = h.shape
    src, rows, group, e_lo, e_hi = tables
    hidden = w_gate.shape[2]
    n_tiles = src.shape[0] // MOE_TILE
    const = lambda i, *_: (0, 0)
    lo_w = lambda i, src, rows, group, elo, ehi: (elo[i], 0, 0)
    hi_w = lambda i, src, rows, group, elo, ehi: (ehi[i], 0, 0)
    grid_spec = pltpu.PrefetchScalarGridSpec(
        num_scalar_prefetch=5,
        grid=(n_tiles,),
        in_specs=[pl.BlockSpec(memory_space=pl.ANY),
                  pl.BlockSpec((1, d), const),
                  pl.BlockSpec((d, LANES), const),
                  pl.BlockSpec((d, LANES), const),
                  pl.BlockSpec((1, LANES), const),
                  pl.BlockSpec((None, d, hidden), lo_w),
                  pl.BlockSpec((None, d, hidden), lo_w),
                  pl.BlockSpec((None, hidden, d), lo_w),
                  pl.BlockSpec((None, d, hidden), hi_w),
                  pl.BlockSpec((None, d, hidden), hi_w),
                  pl.BlockSpec((None, hidden, d), hi_w)],
        out_specs=pl.BlockSpec(memory_space=pl.ANY),
        scratch_shapes=[pltpu.VMEM((MOE_TILE, d), F32),
                        pltpu.VMEM((MOE_TILE, d), F32),
                        pltpu.SemaphoreType.DMA(()),
                        pltpu.SemaphoreType.DMA(())],
    )
    return pl.pallas_call(
        functools.partial(_moe_kernel, n_rows=n),
        grid_spec=grid_spec,
        out_shape=jax.ShapeDtypeStruct((n, d), F32),
        compiler_params=_params("arbitrary"),
        name="sparse_moe",
    )(src, rows, group, e_lo, e_hi, h, norm_w.reshape(1, d), whi, wlo, bias,
      w_gate, w_up, w_down, w_gate, w_up, w_down)


def _ple_kernel(h_ref, p_ref, nw_ref, wg_ref, wp_ref, fw_ref, o_ref, *, final):
    x = h_ref[...]
    gate = jax.nn.sigmoid(_dot(_rms(x, nw_ref[...]).astype(BF16), wg_ref[...]))
    out = x + gate * _dot(p_ref[...].astype(BF16), wp_ref[...])
    if final:
        out = _rms(out, fw_ref[...])
    o_ref[...] = out


def _ple(h, p, norm_w, w_gate, w_proj, final_w, final):
    n = p.shape[0]
    d = h.shape[1]
    const = lambda i: (0, 0)
    return pl.pallas_call(
        functools.partial(_ple_kernel, final=final),
        grid=(n // ROW_TILE,),
        in_specs=[pl.BlockSpec((ROW_TILE, d), lambda i: (i, 0)),
                  pl.BlockSpec((ROW_TILE, p.shape[1]), lambda i: (i, 0)),
                  pl.BlockSpec((1, d), const),
                  pl.BlockSpec(w_gate.shape, const),
                  pl.BlockSpec(w_proj.shape, const),
                  pl.BlockSpec((1, d), const)],
        out_specs=pl.BlockSpec((ROW_TILE, d), lambda i: (i, 0)),
        out_shape=jax.ShapeDtypeStruct((n, d), F32),
        compiler_params=_params("parallel"),
        name="per_layer_embedding",
    )(h, p, norm_w.reshape(1, d), w_gate, w_proj, final_w.reshape(1, d))


def kernel(x, p, attn_norm_w, ffn_norm_w, final_norm_w, even_w_in, even_w_out, ret_norm_w, sg_norm_w, sg_spatial_w, sg_spatial_b, odd_w_in, odd_w_out, moe_w_group, moe_b_group, moe_w_expert, moe_b_expert, moe_w_gate, moe_w_up, moe_w_down, ple_norm_w, ple_w_gate, ple_w_proj):
    batch, seq, d = x.shape
    depth = p.shape[0]
    n = batch * seq
    h = x.reshape(n, d)
    for i in range(depth):
        j = i // 2
        if i % 2 == 0:
            proj = _norm_matmul(h, attn_norm_w[i], even_w_in[j].astype(BF16))
            h = _even_mixer(h, proj, batch, seq, even_w_out[j].astype(BF16), ret_norm_w[j], sg_norm_w[j],
                            sg_spatial_w[j], sg_spatial_b[j])
        else:
            qkv = _norm_matmul(h, attn_norm_w[i], odd_w_in[j].astype(BF16))
            att = _stick_breaking(qkv, batch, seq)
            h = _matmul_residual(att, odd_w_out[j].astype(BF16), h)

        w_route = jnp.concatenate([moe_w_group[i], moe_w_expert[i]], axis=1)
        w_route = jnp.pad(w_route, ((0, 0), (0, LANES - w_route.shape[1])))
        whi, wlo = _split_bf16(w_route)
        bias = jnp.concatenate([moe_b_group[i], moe_b_expert[i]])
        bias = jnp.pad(bias, (0, LANES - bias.shape[0])).reshape(1, LANES)
        bucket = _router(h, ffn_norm_w[i], whi, wlo, bias)
        tables = _route_tables(bucket.reshape(n), n)
        h = _moe(h, ffn_norm_w[i], whi, wlo, bias, moe_w_gate[i].astype(BF16), moe_w_up[i].astype(BF16),
                 moe_w_down[i].astype(BF16), tables)
        h = _ple(h, p[i].reshape(n, -1), ple_norm_w[i], ple_w_gate[i].astype(BF16),
                 ple_w_proj[i].astype(BF16), final_norm_w, final=(i == depth - 1))
    return h.reshape(batch, seq, d)
```

```python
import functools

import jax
import jax.numpy as jnp
from jax import lax
from jax.experimental import pallas as pl
from jax.experimental.pallas import tpu as pltpu

F32 = jnp.float32
BF16 = jnp.bfloat16
I32 = jnp.int32

LANES = 128
CHUNK = 128
RET_HEADS = 4
RET_DIM = 128
RET_WIDTH = RET_HEADS * RET_DIM
SG_GROUPS = 4
SG_DIM = 128
SG_WIDTH = SG_GROUPS * SG_DIM
SB_DIM = 64
MOE_GROUPS = 4
MOE_PER_GROUP = 8
MOE_EXPERTS = MOE_GROUPS * MOE_PER_GROUP
MOE_PAIRS = MOE_PER_GROUP * (MOE_PER_GROUP - 1) // 2
MOE_BUCKETS = MOE_GROUPS * MOE_PAIRS
ROPE_BASE = 10000.0
EPS = 1e-6

ROW_TILE = 512
MIX_TILE = 512
MOE_TILE = 256
ATT_Q = 256
ATT_KV_STEP = 2
EXP_UNDERFLOW = -104.0
DMA_UNROLL = 8
VMEM_LIMIT = 56 * 1024 * 1024


def _params(*sem):
    return pltpu.CompilerParams(dimension_semantics=sem, vmem_limit_bytes=VMEM_LIMIT)


def _rms(x, w):
    return x * lax.rsqrt(jnp.mean(x * x, axis=-1, keepdims=True) + EPS) * w


def _dot(a, b):
    return jnp.dot(a, b, preferred_element_type=F32)


def _dot_nt(a, b):
    return lax.dot_general(a, b, (((1,), (1,)), ((), ())), preferred_element_type=F32)


def _dot_tn(a, b):
    return lax.dot_general(a, b, (((0,), (0,)), ((), ())), preferred_element_type=F32)


def _split_bf16(x):
    hi = x.astype(BF16)
    lo = (x - hi.astype(F32)).astype(BF16)
    return hi, lo


def _norm_matmul_kernel(h_ref, nw_ref, w_ref, o_ref, *, col_chunk, split):
    xn = _rms(h_ref[...], nw_ref[...]).astype(BF16)
    for c in range(0, w_ref.shape[1], col_chunk):
        res = _dot(xn, w_ref[:, c:c + col_chunk]).astype(o_ref.dtype)
        if split:
            for s in range(col_chunk // LANES):
                o_ref[c // LANES + s] = res[:, s * LANES:(s + 1) * LANES]
        else:
            o_ref[:, c:c + col_chunk] = res


def _norm_matmul(h, norm_w, w, split):
    n, d = h.shape
    m = w.shape[1]
    if split:
        out_spec = pl.BlockSpec((m // LANES, ROW_TILE, LANES), lambda i: (0, i, 0))
        out_shape = jax.ShapeDtypeStruct((m // LANES, n, LANES), BF16)
    else:
        out_spec = pl.BlockSpec((ROW_TILE, m), lambda i: (i, 0))
        out_shape = jax.ShapeDtypeStruct((n, m), BF16)
    return pl.pallas_call(
        functools.partial(_norm_matmul_kernel, col_chunk=512, split=split),
        grid=(n // ROW_TILE,),
        in_specs=[pl.BlockSpec((ROW_TILE, d), lambda i: (i, 0)),
                  pl.BlockSpec((1, d), lambda i: (0, 0)),
                  pl.BlockSpec((d, m), lambda i: (0, 0))],
        out_specs=out_spec,
        out_shape=out_shape,
        compiler_params=_params("parallel"),
        name="norm_matmul",
    )(h, norm_w.reshape(1, d), w)


def _matmul_residual_kernel(a_ref, w_ref, h_ref, o_ref):
    a = jnp.concatenate([a_ref[c] for c in range(a_ref.shape[0])], axis=1)
    o_ref[...] = h_ref[...] + _dot(a, w_ref[...])


def _matmul_residual(a, w, h):
    groups, n, _ = a.shape
    d = w.shape[1]
    return pl.pallas_call(
        _matmul_residual_kernel,
        grid=(n // ROW_TILE,),
        in_specs=[pl.BlockSpec((groups, ROW_TILE, LANES), lambda i: (0, i, 0)),
                  pl.BlockSpec(w.shape, lambda i: (0, 0)),
                  pl.BlockSpec((ROW_TILE, d), lambda i: (i, 0))],
        out_specs=pl.BlockSpec((ROW_TILE, d), lambda i: (i, 0)),
        out_shape=jax.ShapeDtypeStruct((n, d), F32),
        compiler_params=_params("parallel"),
        name="matmul_residual",
    )(a, w, h)


def _even_mixer_kernel(h_ref, proj_ref, cos_ref, sin_ref, intra_ref, qdec_ref, kdec_ref, cdec_ref,
                       retw_ref, sgnw_ref, sgw_ref, sgb_ref, wout_ref, o_ref, state_ref, mixed_ref):
    @pl.when(pl.program_id(1) == 0)
    def _():
        state_ref[...] = jnp.zeros_like(state_ref)

    def chunk(c, carry):
        r0 = pl.multiple_of(c * CHUNK, CHUNK)
        rows = pl.ds(r0, CHUNK)
        cos = cos_ref[rows, :]
        sin = sin_ref[rows, :]

        def rope(t):
            return t * cos + pltpu.roll(t, RET_DIM // 2, 1) * sin

        for hd in range(RET_HEADS):
            lanes = slice(hd * RET_DIM, (hd + 1) * RET_DIM)
            q = rope(proj_ref[rows, lanes].astype(F32))
            k = rope(proj_ref[rows, RET_WIDTH + hd * RET_DIM:RET_WIDTH + (hd + 1) * RET_DIM].astype(F32))
            v = proj_ref[rows, 2 * RET_WIDTH + hd * RET_DIM:2 * RET_WIDTH + (hd + 1) * RET_DIM]
            g = proj_ref[rows, 3 * RET_WIDTH + hd * RET_DIM:3 * RET_WIDTH + (hd + 1) * RET_DIM].astype(F32)
            scores = _dot_nt(q.astype(BF16), k.astype(BF16)) * intra_ref[hd]
            st = state_ref[hd]
            out = _dot(scores.astype(BF16), v) + _dot((q * qdec_ref[hd]).astype(BF16), st.astype(BF16))
            state_ref[hd] = st * cdec_ref[hd] + _dot_tn((k * kdec_ref[hd]).astype(BF16), v)
            mu = jnp.mean(out, axis=-1, keepdims=True)
            cen = out - mu
            var = jnp.mean(cen * cen, axis=-1, keepdims=True)
            ret = cen * lax.rsqrt(var + EPS) * retw_ref[:, lanes]
            mixed_ref[rows, lanes] = (jax.nn.silu(g) * ret).astype(BF16)

        base = 4 * RET_WIDTH
        gv = jax.nn.gelu(proj_ref[rows, base + SG_WIDTH:base + 2 * SG_WIDTH].astype(F32))
        vsn = _rms(gv, sgnw_ref[...]).astype(BF16)
        for gi in range(SG_GROUPS):
            lanes = slice(gi * SG_DIM, (gi + 1) * SG_DIM)
            u = proj_ref[rows, base + gi * SG_DIM:base + (gi + 1) * SG_DIM].astype(F32)
            mix = _dot(sgw_ref[gi], vsn[:, lanes]) + sgb_ref[gi]
            mixed_ref[rows, RET_WIDTH + gi * SG_DIM:RET_WIDTH + (gi + 1) * SG_DIM] = (
                jax.nn.gelu(u) * mix).astype(BF16)
        return carry

    lax.fori_loop(0, h_ref.shape[0] // CHUNK, chunk, 0)
    o_ref[...] = h_ref[...] + _dot(mixed_ref[...], wout_ref[...])


def _even_mixer(h, proj, batch, seq, w_out, ret_norm_w, sg_norm_w, sg_w, sg_b):
    n, d = h.shape
    steps = seq // MIX_TILE
    inv_freq = ROPE_BASE ** (-jnp.arange(0, RET_DIM, 2, dtype=F32) / RET_DIM)
    ang = jnp.arange(seq, dtype=F32)[:, None] * inv_freq[None, :]
    cos = jnp.concatenate([jnp.cos(ang), jnp.cos(ang)], axis=-1)
    sin = jnp.concatenate([-jnp.sin(ang), jnp.sin(ang)], axis=-1)
    log_gamma = jnp.log1p(-jnp.exp2(-5.0 - jnp.arange(RET_HEADS, dtype=F32)))
    pos = jnp.arange(CHUNK, dtype=F32)
    diff = pos[:, None] - pos[None, :]
    scale = RET_DIM ** -0.5
    intra = jnp.where(diff >= 0, jnp.exp(log_gamma[:, None, None] * jnp.maximum(diff, 0.0)), 0.0) * scale
    bshape = (RET_HEADS, CHUNK, RET_DIM)
    qdec = jnp.broadcast_to(jnp.exp(log_gamma[:, None] * (pos[None, :] + 1.0))[:, :, None], bshape)
    kdec = jnp.broadcast_to(jnp.exp(log_gamma[:, None] * (CHUNK - 1.0 - pos[None, :]))[:, :, None] * scale, bshape)
    cdec = jnp.broadcast_to(jnp.exp(log_gamma * CHUNK)[:, None, None], bshape)
    tril = jnp.tril(jnp.ones((CHUNK, CHUNK), dtype=bool))
    sgw = jnp.where(tril, sg_w, jnp.zeros_like(sg_w)).astype(BF16)
    sgb = jnp.broadcast_to(sg_b[:, :, None], (SG_GROUPS, CHUNK, SG_DIM))

    const3 = lambda b, s: (0, 0, 0)
    const2 = lambda b, s: (0, 0)
    row = lambda b, s: (b * steps + s, 0)
    return pl.pallas_call(
        _even_mixer_kernel,
        grid=(batch, steps),
        in_specs=[pl.BlockSpec((MIX_TILE, d), row),
                  pl.BlockSpec((MIX_TILE, proj.shape[1]), row),
                  pl.BlockSpec((MIX_TILE, RET_DIM), lambda b, s: (s, 0)),
                  pl.BlockSpec((MIX_TILE, RET_DIM), lambda b, s: (s, 0)),
                  pl.BlockSpec(bshape, const3),
                  pl.BlockSpec(bshape, const3),
                  pl.BlockSpec(bshape, const3),
                  pl.BlockSpec(bshape, const3),
                  pl.BlockSpec((1, RET_WIDTH), const2),
                  pl.BlockSpec((1, SG_WIDTH), const2),
                  pl.BlockSpec((SG_GROUPS, CHUNK, CHUNK), const3),
                  pl.BlockSpec((SG_GROUPS, CHUNK, SG_DIM), const3),
                  pl.BlockSpec(w_out.shape, const2)],
        out_specs=pl.BlockSpec((MIX_TILE, d), row),
        out_shape=jax.ShapeDtypeStruct((n, d), F32),
        scratch_shapes=[pltpu.VMEM((RET_HEADS, RET_DIM, RET_DIM), F32),
                        pltpu.VMEM((MIX_TILE, RET_WIDTH + SG_WIDTH), BF16)],
        compiler_params=_params("parallel", "arbitrary"),
        name="even_mixer",
    )(h, proj, cos, sin, intra, qdec, kdec, cdec, ret_norm_w.reshape(1, -1), sg_norm_w.reshape(1, -1),
      sgw, sgb, w_out)


def _stick_kernel(q_ref, k_ref, v_ref, cum_ref, o_ref, qs_ref):
    seq = q_ref.shape[0]
    n_sub = ATT_Q // CHUNK
    row = lax.broadcasted_iota(I32, (CHUNK, CHUNK), 0)
    lane = lax.broadcasted_iota(I32, (CHUNK, CHUNK), 1)
    tri = lane < row
    first_k = lane < SB_DIM
    first_q = lax.broadcasted_iota(I32, (ATT_Q, LANES), 1) < SB_DIM
    cum = cum_ref[...]

    def part(z, carry, v_head, masked):
        stay = jnp.minimum(-z, 0.0) - jnp.log(1.0 + jnp.exp(-jnp.abs(z)))
        if masked:
            stay = jnp.where(tri, stay, 0.0)
        hi, lo = _split_bf16(stay)
        sums = _dot(jnp.concatenate([hi, lo], axis=1), cum)
        a = jnp.exp(z + carry + sums[:, :CHUNK])
        if masked:
            a = jnp.where(tri, a, 0.0)
        return carry + sums[:, CHUNK:], _dot(a.astype(BF16), v_head)

    def load_kv(j):
        keys = pl.ds(pl.multiple_of(j * CHUNK, CHUNK), CHUNK)
        kb = k_ref[keys, :]
        vb = v_ref[keys, :]
        zero = jnp.zeros_like(vb)
        return kb, (jnp.where(first_k, vb, zero), jnp.where(first_k, zero, vb))

    def tile(t, c):
        r0 = pl.multiple_of(t * ATT_Q, ATT_Q)
        q = q_ref[pl.ds(r0, ATT_Q), :] * (SB_DIM ** -0.5)
        zero = jnp.zeros_like(q)
        qs_ref[0:ATT_Q, :] = jnp.where(first_q, q, zero)
        qs_ref[ATT_Q:, :] = jnp.where(first_q, zero, q)
        blank = jnp.zeros((CHUNK, CHUNK), F32)
        carry = [blank] * (2 * n_sub)
        acc = [blank] * n_sub

        for jd in reversed(range(n_sub)):
            kb, v_heads = load_kv(t * n_sub + jd)
            for s in range(jd, n_sub):
                for head in range(2):
                    r = head * ATT_Q + s * CHUNK
                    z = _dot_nt(qs_ref[r:r + CHUNK, :], kb)
                    carry[head * n_sub + s], o = part(z, carry[head * n_sub + s], v_heads[head], s == jd)
                    acc[s] = acc[s] + o

        def older(state):
            it, _, carry, acc = state
            carry, acc = list(carry), list(acc)
            for b in range(ATT_KV_STEP):
                kb, v_heads = load_kv(t * n_sub - 1 - it * ATT_KV_STEP - b)
                z_all = _dot_nt(qs_ref[...], kb)
                for s in range(n_sub):
                    for head in range(2):
                        r = head * ATT_Q + s * CHUNK
                        carry[head * n_sub + s], o = part(z_all[r:r + CHUNK, :], carry[head * n_sub + s],
                                                          v_heads[head], False)
                        acc[s] = acc[s] + o
            top = carry[0]
            for other in carry[1:]:
                top = jnp.maximum(top, other)
            return it + 1, jnp.max(top), tuple(carry), tuple(acc)

        def unfinished(state):
            return (state[0] < (t * n_sub) // ATT_KV_STEP) & (state[1] > EXP_UNDERFLOW)

        start = (jnp.int32(0), jnp.float32(0.0), tuple(carry), tuple(acc))
        acc = lax.while_loop(unfinished, older, start)[3]
        for s in range(n_sub):
            o_ref[pl.ds(r0 + s * CHUNK, CHUNK), :] = acc[s].astype(o_ref.dtype)
        return c

    lax.fori_loop(0, seq // ATT_Q, tile, 0)


def _stick_breaking(qkv, batch, seq):
    pairs = qkv.shape[0] // 3
    n = qkv.shape[1]
    tri = (jnp.arange(CHUNK)[:, None] >= jnp.arange(CHUNK)[None, :])
    cum = jnp.concatenate([tri.astype(BF16), jnp.ones((CHUNK, CHUNK), BF16)], axis=1)
    cum = jnp.concatenate([cum, cum], axis=0)
    blk = (None, seq, LANES)
    return pl.pallas_call(
        _stick_kernel,
        grid=(batch, pairs),
        in_specs=[pl.BlockSpec(blk, lambda b, p: (p, b, 0)),
                  pl.BlockSpec(blk, lambda b, p: (pairs + p, b, 0)),
                  pl.BlockSpec(blk, lambda b, p: (2 * pairs + p, b, 0)),
                  pl.BlockSpec((2 * CHUNK, 2 * CHUNK), lambda b, p: (0, 0))],
        out_specs=pl.BlockSpec(blk, lambda b, p: (p, b, 0)),
        out_shape=jax.ShapeDtypeStruct((pairs, n, LANES), BF16),
        scratch_shapes=[pltpu.VMEM((2 * ATT_Q, LANES), BF16)],
        compiler_params=_params("parallel", "parallel"),
        name="stick_breaking",
    )(qkv, qkv, qkv, cum)


def _router_logits(xn, whi_ref, wlo_ref, b_ref):
    hi, lo = _split_bf16(xn)
    return _dot(hi, whi_ref[...]) + _dot(hi, wlo_ref[...]) + _dot(lo, whi_ref[...]) + b_ref[...]


def _router_kernel(h_ref, nw_ref, whi_ref, wlo_ref, b_ref, tri_ref, bucket_ref, rank_ref, counts_ref, run_ref):
    @pl.when(pl.program_id(0) == 0)
    def _():
        run_ref[...] = jnp.zeros_like(run_ref)

    logits = _router_logits(_rms(h_ref[...], nw_ref[...]), whi_ref, wlo_ref, b_ref)
    lane = lax.broadcasted_iota(I32, logits.shape, 1)
    neg = jnp.float32(-jnp.inf)

    def first_max(vals):
        m = jnp.max(vals, axis=-1, keepdims=True)
        return jnp.min(jnp.where(vals == m, lane, LANES), axis=-1, keepdims=True)

    group = first_max(jnp.where(lane < MOE_GROUPS, logits, neg))
    rel = lane - MOE_GROUPS
    in_group = (rel >= 0) & (rel < MOE_EXPERTS) & ((rel >> 3) == group)
    ev = jnp.where(in_group, logits, neg)
    i1 = first_max(ev)
    i2 = first_max(jnp.where(lane == i1, neg, ev))
    a = (i1 - MOE_GROUPS) & (MOE_PER_GROUP - 1)
    b = (i2 - MOE_GROUPS) & (MOE_PER_GROUP - 1)
    lo = jnp.minimum(a, b)
    hi = jnp.maximum(a, b)
    pair = (lo * (2 * MOE_PER_GROUP - 1 - lo)) // 2 + (hi - lo - 1)
    bucket = group * MOE_PAIRS + pair

    onehot = lane == bucket
    before = _dot(tri_ref[...], onehot.astype(BF16)) + run_ref[...]
    rank = jnp.sum(jnp.where(onehot, before, 0.0), axis=-1, keepdims=True)
    run_ref[...] += jnp.sum(onehot.astype(F32), axis=0, keepdims=True)
    bucket_ref[...] = bucket
    rank_ref[...] = rank.astype(I32)
    counts_ref[...] = run_ref[...]


def _router(h, norm_w, whi, wlo, bias):
    n, d = h.shape
    const = lambda i: (0, 0)
    tri = (jnp.arange(ROW_TILE)[:, None] > jnp.arange(ROW_TILE)[None, :]).astype(BF16)
    col = jax.ShapeDtypeStruct((n, 1), I32)
    return pl.pallas_call(
        _router_kernel,
        grid=(n // ROW_TILE,),
        in_specs=[pl.BlockSpec((ROW_TILE, d), lambda i: (i, 0)),
                  pl.BlockSpec((1, d), const),
                  pl.BlockSpec((d, LANES), const),
                  pl.BlockSpec((d, LANES), const),
                  pl.BlockSpec((1, LANES), const),
                  pl.BlockSpec((ROW_TILE, ROW_TILE), const)],
        out_specs=[pl.BlockSpec((ROW_TILE, 1), lambda i: (i, 0)),
                   pl.BlockSpec((ROW_TILE, 1), lambda i: (i, 0)),
                   pl.BlockSpec((1, LANES), const)],
        out_shape=[col, col, jax.ShapeDtypeStruct((1, LANES), F32)],
        scratch_shapes=[pltpu.VMEM((1, LANES), F32)],
        compiler_params=_params("arbitrary"),
        name="moe_router",
    )(h, norm_w.reshape(1, d), whi, wlo, bias, tri)


def _slot_kernel(bucket_ref, rank_ref, start_ref, slot_ref):
    lane = lax.broadcasted_iota(I32, (bucket_ref.shape[0], LANES), 1)
    start = jnp.sum(jnp.where(lane == bucket_ref[...], start_ref[...], 0), axis=-1, keepdims=True)
    slot_ref[...] = start + rank_ref[...]


def _slots(bucket, rank, start):
    n = bucket.shape[0]
    rows = ROW_TILE
    return pl.pallas_call(
        _slot_kernel,
        grid=(n // rows,),
        in_specs=[pl.BlockSpec((rows, 1), lambda i: (i, 0)),
                  pl.BlockSpec((rows, 1), lambda i: (i, 0)),
                  pl.BlockSpec((1, LANES), lambda i: (0, 0))],
        out_specs=pl.BlockSpec((rows, 1), lambda i: (i, 0)),
        out_shape=jax.ShapeDtypeStruct((n, 1), I32),
        compiler_params=_params("parallel"),
        name="moe_slots",
    )(bucket, rank, start)


def _tile_tables(counts, n):
    t = MOE_TILE
    n_tiles = n // t + MOE_BUCKETS
    counts = counts.reshape(LANES).astype(I32)[:MOE_BUCKETS]
    tiles = (counts + t - 1) // t
    tile_end = jnp.cumsum(tiles)
    tile_start = tile_end - tiles
    used = tile_end[-1]
    tile_id = jnp.arange(n_tiles, dtype=I32)
    block = jnp.minimum(tile_id, used - 1)
    tile_bucket = jnp.sum((tile_end[None, :] <= block[:, None]).astype(I32), axis=1)
    onehot = (tile_bucket[:, None] == jnp.arange(MOE_BUCKETS, dtype=I32)[None, :]).astype(I32)
    bucket_count = jnp.sum(onehot * counts[None, :], axis=1)
    bucket_start = jnp.sum(onehot * tile_start[None, :], axis=1)
    rows = jnp.clip(bucket_count - (block - bucket_start) * t, 0, t)
    rows = jnp.where(tile_id < used, rows, 0)
    group = tile_bucket // MOE_PAIRS
    pair = tile_bucket % MOE_PAIRS
    lo_tab, hi_tab = [], []
    for lo in range(MOE_PER_GROUP):
        for hi in range(lo + 1, MOE_PER_GROUP):
            lo_tab.append(lo)
            hi_tab.append(hi)
    pair_hot = (pair[:, None] == jnp.arange(MOE_PAIRS, dtype=I32)[None, :]).astype(I32)
    e_lo = group * MOE_PER_GROUP + jnp.sum(pair_hot * jnp.asarray(lo_tab, I32)[None, :], axis=1)
    e_hi = group * MOE_PER_GROUP + jnp.sum(pair_hot * jnp.asarray(hi_tab, I32)[None, :], axis=1)
    start = jnp.pad(tile_start * t, (0, LANES - MOE_BUCKETS)).reshape(1, LANES)
    return start, (block, rows.astype(I32), group, e_lo, e_hi)


def _dispatch_kernel(slot_ref, h_ref, init_hbm, out_hbm, sem):
    del init_hbm
    rows = h_ref.shape[0]
    base = pl.program_id(0) * rows

    def issue(r, c):
        pltpu.make_async_copy(h_ref.at[r], out_hbm.at[slot_ref[base + r]], sem).start()
        return c

    lax.fori_loop(0, rows, issue, 0, unroll=DMA_UNROLL)
    pltpu.make_async_copy(h_ref, out_hbm.at[pl.ds(0, rows)], sem).wait()


def _dispatch(h, slot, n_slots):
    n, d = h.shape
    grid_spec = pltpu.PrefetchScalarGridSpec(
        num_scalar_prefetch=1,
        grid=(n // ROW_TILE,),
        in_specs=[pl.BlockSpec((ROW_TILE, d), lambda i, slot: (i, 0)),
                  pl.BlockSpec(memory_space=pl.ANY)],
        out_specs=pl.BlockSpec(memory_space=pl.ANY),
        scratch_shapes=[pltpu.SemaphoreType.DMA(())],
    )
    return pl.pallas_call(
        _dispatch_kernel,
        grid_spec=grid_spec,
        out_shape=jax.ShapeDtypeStruct((n_slots, d), F32),
        input_output_aliases={2: 0},
        compiler_params=_params("arbitrary"),
        name="moe_dispatch",
    )(slot, h, jnp.zeros((n_slots, d), F32))


def _moe_kernel(block_ref, rows_ref, group_ref, elo_ref, ehi_ref,
                x_ref, nw_ref, whi_ref, wlo_ref, b_ref, wg1_ref, wu1_ref, wd1_ref, wg2_ref, wu2_ref, wd2_ref,
                o_ref):
    i = pl.program_id(0)

    @pl.when(rows_ref[i] == 0)
    def _():
        o_ref[...] = jnp.zeros_like(o_ref)

    @pl.when(rows_ref[i] > 0)
    def _():
        x = x_ref[...]
        xn = _rms(x, nw_ref[...])
        logits = _router_logits(xn, whi_ref, wlo_ref, b_ref)
        lane = lax.broadcasted_iota(I32, logits.shape, 1)

        def pick(idx):
            return jnp.sum(jnp.where(lane == idx, logits, 0.0), axis=-1, keepdims=True)

        lg = pick(group_ref[i])
        gexp = jnp.exp(jnp.where(lane < MOE_GROUPS, logits - lg, -jnp.inf))
        group_w = 1.0 / jnp.sum(gexp, axis=-1, keepdims=True)
        l1 = pick(MOE_GROUPS + elo_ref[i])
        l2 = pick(MOE_GROUPS + ehi_ref[i])
        m = jnp.maximum(l1, l2)
        e1 = jnp.exp(l1 - m)
        e2 = jnp.exp(l2 - m)
        w1 = e1 / (e1 + e2) * group_w
        w2 = e2 / (e1 + e2) * group_w

        xb = xn.astype(BF16)
        hid1 = jax.nn.silu(_dot(xb, wg1_ref[...])) * _dot(xb, wu1_ref[...]) * w1
        hid2 = jax.nn.silu(_dot(xb, wg2_ref[...])) * _dot(xb, wu2_ref[...]) * w2
        o_ref[...] = x + _dot(hid1.astype(BF16), wd1_ref[...]) + _dot(hid2.astype(BF16), wd2_ref[...])


def _moe(xs, norm_w, whi, wlo, bias, w_gate, w_up, w_down, tables):
    n_slots, d = xs.shape
    hidden = w_gate.shape[2]
    const = lambda i, *_: (0, 0)
    tile = lambda i, block, rows, group, elo, ehi: (block[i], 0)
    lo_w = lambda i, block, rows, group, elo, ehi: (elo[i], 0, 0)
    hi_w = lambda i, block, rows, group, elo, ehi: (ehi[i], 0, 0)
    grid_spec = pltpu.PrefetchScalarGridSpec(
        num_scalar_prefetch=5,
        grid=(n_slots // MOE_TILE,),
        in_specs=[pl.BlockSpec((MOE_TILE, d), tile),
                  pl.BlockSpec((1, d), const),
                  pl.BlockSpec((d, LANES), const),
                  pl.BlockSpec((d, LANES), const),
                  pl.BlockSpec((1, LANES), const),
                  pl.BlockSpec((None, d, hidden), lo_w),
                  pl.BlockSpec((None, d, hidden), lo_w),
                  pl.BlockSpec((None, hidden, d), lo_w),
                  pl.BlockSpec((None, d, hidden), hi_w),
                  pl.BlockSpec((None, d, hidden), hi_w),
                  pl.BlockSpec((None, hidden, d), hi_w)],
        out_specs=pl.BlockSpec((MOE_TILE, d), lambda i, *_: (i, 0)),
    )
    return pl.pallas_call(
        _moe_kernel,
        grid_spec=grid_spec,
        out_shape=jax.ShapeDtypeStruct((n_slots, d), F32),
        compiler_params=_params("arbitrary"),
        name="sparse_moe",
    )(*tables, xs, norm_w.reshape(1, d), whi, wlo, bias, w_gate, w_up, w_down, w_gate, w_up, w_down)


def _ple_kernel(slot_ref, ys_hbm, p_ref, nw_ref, wg_ref, wp_ref, fw_ref, o_ref, buf, sem, *, final):
    i = pl.program_id(0)
    rows = o_ref.shape[0]

    def fetch(tile, which):
        def issue(r, c):
            pltpu.make_async_copy(ys_hbm.at[slot_ref[tile * rows + r]], buf.at[which, r], sem.at[which]).start()
            return c

        lax.fori_loop(0, rows, issue, 0, unroll=DMA_UNROLL)

    @pl.when(i == 0)
    def _():
        fetch(0, 0)

    @pl.when(i + 1 < pl.num_programs(0))
    def _():
        fetch(i + 1, (i + 1) % 2)

    which = i % 2
    pltpu.make_async_copy(ys_hbm.at[pl.ds(0, rows)], buf.at[which], sem.at[which]).wait()
    x = buf[which]
    gate = jax.nn.sigmoid(_dot(_rms(x, nw_ref[...]).astype(BF16), wg_ref[...]))
    out = x + gate * _dot(p_ref[...].astype(BF16), wp_ref[...])
    if final:
        out = _rms(out, fw_ref[...])
    o_ref[...] = out


def _ple(ys, slot, p, norm_w, w_gate, w_proj, final_w, final):
    n = p.shape[0]
    d = ys.shape[1]
    const = lambda i, slot: (0, 0)
    grid_spec = pltpu.PrefetchScalarGridSpec(
        num_scalar_prefetch=1,
        grid=(n // ROW_TILE,),
        in_specs=[pl.BlockSpec(memory_space=pl.ANY),
                  pl.BlockSpec((ROW_TILE, p.shape[1]), lambda i, slot: (i, 0)),
                  pl.BlockSpec((1, d), const),
                  pl.BlockSpec(w_gate.shape, const),
                  pl.BlockSpec(w_proj.shape, const),
                  pl.BlockSpec((1, d), const)],
        out_specs=pl.BlockSpec((ROW_TILE, d), lambda i, slot: (i, 0)),
        scratch_shapes=[pltpu.VMEM((2, ROW_TILE, d), F32),
                        pltpu.SemaphoreType.DMA((2,))],
    )
    return pl.pallas_call(
        functools.partial(_ple_kernel, final=final),
        grid_spec=grid_spec,
        out_shape=jax.ShapeDtypeStruct((n, d), F32),
        compiler_params=_params("arbitrary"),
        name="per_layer_embedding",
    )(slot, ys, p, norm_w.reshape(1, d), w_gate, w_proj, final_w.reshape(1, d))


def kernel(x, p, attn_norm_w, ffn_norm_w, final_norm_w, even_w_in, even_w_out, ret_norm_w, sg_norm_w, sg_spatial_w, sg_spatial_b, odd_w_in, odd_w_out, moe_w_group, moe_b_group, moe_w_expert, moe_b_expert, moe_w_gate, moe_w_up, moe_w_down, ple_norm_w, ple_w_gate, ple_w_proj):
    batch, seq, d = x.shape
    depth = p.shape[0]
    n = batch * seq
    n_slots = n + MOE_BUCKETS * MOE_TILE
    h = x.reshape(n, d)
    for i in range(depth):
        j = i // 2
        if i % 2 == 0:
            proj = _norm_matmul(h, attn_norm_w[i], even_w_in[j].astype(BF16), split=False)
            h = _even_mixer(h, proj, batch, seq, even_w_out[j].astype(BF16), ret_norm_w[j], sg_norm_w[j],
                            sg_spatial_w[j], sg_spatial_b[j])
        else:
            qkv = _norm_matmul(h, attn_norm_w[i], odd_w_in[j].astype(BF16), split=True)
            att = _stick_breaking(qkv, batch, seq)
            h = _matmul_residual(att, odd_w_out[j].astype(BF16), h)

        w_route = jnp.concatenate([moe_w_group[i], moe_w_expert[i]], axis=1)
        w_route = jnp.pad(w_route, ((0, 0), (0, LANES - w_route.shape[1])))
        whi, wlo = _split_bf16(w_route)
        bias = jnp.concatenate([moe_b_group[i], moe_b_expert[i]])
        bias = jnp.pad(bias, (0, LANES - bias.shape[0])).reshape(1, LANES)
        bucket, rank, counts = _router(h, ffn_norm_w[i], whi, wlo, bias)
        start, tables = _tile_tables(counts, n)
        slot = _slots(bucket, rank, start).reshape(n)
        xs = _dispatch(h, slot, n_slots)
        ys = _moe(xs, ffn_norm_w[i], whi, wlo, bias, moe_w_gate[i].astype(BF16), moe_w_up[i].astype(BF16),
                  moe_w_down[i].astype(BF16), tables)
        h = _ple(ys, slot, p[i].reshape(n, -1), ple_norm_w[i], ple_w_gate[i].astype(BF16),
                 ple_w_proj[i].astype(BF16), final_norm_w, final=(i == depth - 1))
    return h.reshape(batch, seq, d)
```

```python
import functools

import jax
import jax.numpy as jnp
from jax import lax
from jax.experimental import pallas as pl
from jax.experimental.pallas import tpu as pltpu

F32 = jnp.float32
BF16 = jnp.bfloat16
I32 = jnp.int32

LANES = 128
CHUNK = 128
RET_HEADS = 4
RET_DIM = 128
RET_WIDTH = RET_HEADS * RET_DIM
SG_GROUPS = 4
SG_DIM = 128
SG_WIDTH = SG_GROUPS * SG_DIM
SB_DIM = 64
MOE_GROUPS = 4
MOE_PER_GROUP = 8
MOE_EXPERTS = MOE_GROUPS * MOE_PER_GROUP
MOE_PAIRS = MOE_PER_GROUP * (MOE_PER_GROUP - 1) // 2
MOE_BUCKETS = MOE_GROUPS * MOE_PAIRS
ROPE_BASE = 10000.0
EPS = 1e-6

ROW_TILE = 512
MIX_TILE = 512
MOE_TILE = 256
ATT_Q = 256
ATT_KV_STEP = 2
EXP_UNDERFLOW = -104.0
DMA_UNROLL = 8
VMEM_LIMIT = 56 * 1024 * 1024


def _params(*sem):
    return pltpu.CompilerParams(dimension_semantics=sem, vmem_limit_bytes=VMEM_LIMIT)


def _rms(x, w):
    return x * lax.rsqrt(jnp.mean(x * x, axis=-1, keepdims=True) + EPS) * w


def _dot(a, b):
    return jnp.dot(a, b, preferred_element_type=F32)


def _dot_nt(a, b):
    return lax.dot_general(a, b, (((1,), (1,)), ((), ())), preferred_element_type=F32)


def _dot_tn(a, b):
    return lax.dot_general(a, b, (((0,), (0,)), ((), ())), preferred_element_type=F32)


def _aligned(index, multiple):
    return index if isinstance(index, int) else pl.multiple_of(index, multiple)


def _split_bf16(x):
    hi = x.astype(BF16)
    lo = (x - hi.astype(F32)).astype(BF16)
    return hi, lo


def _norm_matmul_kernel(h_ref, nw_ref, w_ref, o_ref, *, col_chunk, split):
    xn = _rms(h_ref[...], nw_ref[...]).astype(BF16)
    for c in range(0, w_ref.shape[1], col_chunk):
        res = _dot(xn, w_ref[:, c:c + col_chunk]).astype(o_ref.dtype)
        if split:
            for s in range(col_chunk // LANES):
                o_ref[c // LANES + s] = res[:, s * LANES:(s + 1) * LANES]
        else:
            o_ref[:, c:c + col_chunk] = res


def _norm_matmul(h, norm_w, w, split):
    n, d = h.shape
    m = w.shape[1]
    if split:
        out_spec = pl.BlockSpec((m // LANES, ROW_TILE, LANES), lambda i: (0, i, 0))
        out_shape = jax.ShapeDtypeStruct((m // LANES, n, LANES), BF16)
    else:
        out_spec = pl.BlockSpec((ROW_TILE, m), lambda i: (i, 0))
        out_shape = jax.ShapeDtypeStruct((n, m), BF16)
    return pl.pallas_call(
        functools.partial(_norm_matmul_kernel, col_chunk=512, split=split),
        grid=(n // ROW_TILE,),
        in_specs=[pl.BlockSpec((ROW_TILE, d), lambda i: (i, 0)),
                  pl.BlockSpec((1, d), lambda i: (0, 0)),
                  pl.BlockSpec((d, m), lambda i: (0, 0))],
        out_specs=out_spec,
        out_shape=out_shape,
        compiler_params=_params("parallel"),
        name="norm_matmul",
    )(h, norm_w.reshape(1, d), w)


def _matmul_residual_kernel(a_ref, w_ref, h_ref, o_ref):
    a = jnp.concatenate([a_ref[c] for c in range(a_ref.shape[0])], axis=1)
    o_ref[...] = h_ref[...] + _dot(a, w_ref[...])


def _matmul_residual(a, w, h):
    groups, n, _ = a.shape
    d = w.shape[1]
    return pl.pallas_call(
        _matmul_residual_kernel,
        grid=(n // ROW_TILE,),
        in_specs=[pl.BlockSpec((groups, ROW_TILE, LANES), lambda i: (0, i, 0)),
                  pl.BlockSpec(w.shape, lambda i: (0, 0)),
                  pl.BlockSpec((ROW_TILE, d), lambda i: (i, 0))],
        out_specs=pl.BlockSpec((ROW_TILE, d), lambda i: (i, 0)),
        out_shape=jax.ShapeDtypeStruct((n, d), F32),
        compiler_params=_params("parallel"),
        name="matmul_residual",
    )(a, w, h)


def _even_mixer_kernel(h_ref, proj_ref, cos_ref, sin_ref, intra_ref, qdec_ref, kdec_ref, cdec_ref,
                       retw_ref, sgnw_ref, sgw_ref, sgb_ref, wout_ref, o_ref, state_ref, mixed_ref):
    @pl.when(pl.program_id(1) == 0)
    def _():
        state_ref[...] = jnp.zeros_like(state_ref)

    def chunk(c, carry):
        r0 = pl.multiple_of(c * CHUNK, CHUNK)
        rows = pl.ds(r0, CHUNK)
        cos = cos_ref[rows, :]
        sin = sin_ref[rows, :]

        def rope(t):
            return t * cos + pltpu.roll(t, RET_DIM // 2, 1) * sin

        heads = range(RET_HEADS)
        lanes = [slice(hd * RET_DIM, (hd + 1) * RET_DIM) for hd in heads]

        def field(idx, hd):
            return proj_ref[rows, idx * RET_WIDTH + hd * RET_DIM:idx * RET_WIDTH + (hd + 1) * RET_DIM]

        q = [rope(field(0, hd).astype(F32)) for hd in heads]
        k = [rope(field(1, hd).astype(F32)) for hd in heads]
        v = [field(2, hd) for hd in heads]
        scores = [_dot_nt(q[hd].astype(BF16), k[hd].astype(BF16)) * intra_ref[hd] for hd in heads]
        st = [state_ref[hd] for hd in heads]
        out = [_dot(scores[hd].astype(BF16), v[hd]) + _dot((q[hd] * qdec_ref[hd]).astype(BF16), st[hd].astype(BF16))
               for hd in heads]
        for hd in heads:
            state_ref[hd] = st[hd] * cdec_ref[hd] + _dot_tn((k[hd] * kdec_ref[hd]).astype(BF16), v[hd])
        for hd in heads:
            mu = jnp.mean(out[hd], axis=-1, keepdims=True)
            cen = out[hd] - mu
            var = jnp.mean(cen * cen, axis=-1, keepdims=True)
            ret = cen * lax.rsqrt(var + EPS) * retw_ref[:, lanes[hd]]
            mixed_ref[rows, lanes[hd]] = (jax.nn.silu(field(3, hd).astype(F32)) * ret).astype(BF16)

        base = 4 * RET_WIDTH
        gv = jax.nn.gelu(proj_ref[rows, base + SG_WIDTH:base + 2 * SG_WIDTH].astype(F32))
        vsn = _rms(gv, sgnw_ref[...]).astype(BF16)
        for gi in range(SG_GROUPS):
            lanes = slice(gi * SG_DIM, (gi + 1) * SG_DIM)
            u = proj_ref[rows, base + gi * SG_DIM:base + (gi + 1) * SG_DIM].astype(F32)
            mix = _dot(sgw_ref[gi], vsn[:, lanes]) + sgb_ref[gi]
            mixed_ref[rows, RET_WIDTH + gi * SG_DIM:RET_WIDTH + (gi + 1) * SG_DIM] = (
                jax.nn.gelu(u) * mix).astype(BF16)
        return carry

    lax.fori_loop(0, h_ref.shape[0] // CHUNK, chunk, 0)
    o_ref[...] = h_ref[...] + _dot(mixed_ref[...], wout_ref[...])


def _even_mixer(h, proj, batch, seq, w_out, ret_norm_w, sg_norm_w, sg_w, sg_b):
    n, d = h.shape
    steps = seq // MIX_TILE
    inv_freq = ROPE_BASE ** (-jnp.arange(0, RET_DIM, 2, dtype=F32) / RET_DIM)
    ang = jnp.arange(seq, dtype=F32)[:, None] * inv_freq[None, :]
    cos = jnp.concatenate([jnp.cos(ang), jnp.cos(ang)], axis=-1)
    sin = jnp.concatenate([-jnp.sin(ang), jnp.sin(ang)], axis=-1)
    log_gamma = jnp.log1p(-jnp.exp2(-5.0 - jnp.arange(RET_HEADS, dtype=F32)))
    pos = jnp.arange(CHUNK, dtype=F32)
    diff = pos[:, None] - pos[None, :]
    scale = RET_DIM ** -0.5
    intra = jnp.where(diff >= 0, jnp.exp(log_gamma[:, None, None] * jnp.maximum(diff, 0.0)), 0.0) * scale
    bshape = (RET_HEADS, CHUNK, RET_DIM)
    qdec = jnp.broadcast_to(jnp.exp(log_gamma[:, None] * (pos[None, :] + 1.0))[:, :, None], bshape)
    kdec = jnp.broadcast_to(jnp.exp(log_gamma[:, None] * (CHUNK - 1.0 - pos[None, :]))[:, :, None] * scale, bshape)
    cdec = jnp.broadcast_to(jnp.exp(log_gamma * CHUNK)[:, None, None], bshape)
    tril = jnp.tril(jnp.ones((CHUNK, CHUNK), dtype=bool))
    sgw = jnp.where(tril, sg_w, jnp.zeros_like(sg_w)).astype(BF16)
    sgb = jnp.broadcast_to(sg_b[:, :, None], (SG_GROUPS, CHUNK, SG_DIM))

    const3 = lambda b, s: (0, 0, 0)
    const2 = lambda b, s: (0, 0)
    row = lambda b, s: (b * steps + s, 0)
    return pl.pallas_call(
        _even_mixer_kernel,
        grid=(batch, steps),
        in_specs=[pl.BlockSpec((MIX_TILE, d), row),
                  pl.BlockSpec((MIX_TILE, proj.shape[1]), row),
                  pl.BlockSpec((MIX_TILE, RET_DIM), lambda b, s: (s, 0)),
                  pl.BlockSpec((MIX_TILE, RET_DIM), lambda b, s: (s, 0)),
                  pl.BlockSpec(bshape, const3),
                  pl.BlockSpec(bshape, const3),
                  pl.BlockSpec(bshape, const3),
                  pl.BlockSpec(bshape, const3),
                  pl.BlockSpec((1, RET_WIDTH), const2),
                  pl.BlockSpec((1, SG_WIDTH), const2),
                  pl.BlockSpec((SG_GROUPS, CHUNK, CHUNK), const3),
                  pl.BlockSpec((SG_GROUPS, CHUNK, SG_DIM), const3),
                  pl.BlockSpec(w_out.shape, const2)],
        out_specs=pl.BlockSpec((MIX_TILE, d), row),
        out_shape=jax.ShapeDtypeStruct((n, d), F32),
        scratch_shapes=[pltpu.VMEM((RET_HEADS, RET_DIM, RET_DIM), F32),
                        pltpu.VMEM((MIX_TILE, RET_WIDTH + SG_WIDTH), BF16)],
        compiler_params=_params("parallel", "arbitrary"),
        name="even_mixer",
    )(h, proj, cos, sin, intra, qdec, kdec, cdec, ret_norm_w.reshape(1, -1), sg_norm_w.reshape(1, -1),
      sgw, sgb, w_out)


def _stick_kernel(q_ref, k_ref, v_ref, cum_ref, o_ref, qs_ref):
    seq = q_ref.shape[0]
    n_sub = ATT_Q // CHUNK
    row = lax.broadcasted_iota(I32, (CHUNK, CHUNK), 0)
    lane = lax.broadcasted_iota(I32, (CHUNK, CHUNK), 1)
    tri = lane < row
    first_k = lane < SB_DIM
    first_q = lax.broadcasted_iota(I32, (ATT_Q, LANES), 1) < SB_DIM
    cum = cum_ref[...]

    def run(jobs, carry, acc):
        stacked = []
        for z, _, _, diagonal, _ in jobs:
            stay = jnp.minimum(-z, 0.0) - jnp.log(1.0 + jnp.exp(-jnp.abs(z)))
            if diagonal:
                stay = jnp.where(tri, stay, 0.0)
            stacked.append(jnp.concatenate(_split_bf16(stay), axis=1))
        sums = [_dot(lhs, cum) for lhs in stacked]
        weights = []
        for (z, s, head, diagonal, _), both in zip(jobs, sums):
            c = head * n_sub + s
            a = jnp.exp(z + carry[c] + both[:, :CHUNK])
            if diagonal:
                a = jnp.where(tri, a, 0.0)
            carry[c] = carry[c] + both[:, CHUNK:]
            weights.append(a.astype(BF16))
        for (_, s, _, _, v_head), a in zip(jobs, weights):
            acc[s] = acc[s] + _dot(a, v_head)

    def load_kv(j):
        keys = pl.ds(_aligned(j * CHUNK, CHUNK), CHUNK)
        kb = k_ref[keys, :]
        vb = v_ref[keys, :]
        zero = jnp.zeros_like(vb)
        return kb, (jnp.where(first_k, vb, zero), jnp.where(first_k, zero, vb))

    def older_jobs(t, trip):
        jobs = []
        for b in range(ATT_KV_STEP):
            kb, v_heads = load_kv(t * n_sub - 1 - trip * ATT_KV_STEP - b)
            z_all = _dot_nt(qs_ref[...], kb)
            for s in range(n_sub):
                for head in range(2):
                    r = head * ATT_Q + s * CHUNK
                    jobs.append((z_all[r:r + CHUNK, :], s, head, False, v_heads[head]))
        return jobs

    def tile(t, first):
        r0 = _aligned(t * ATT_Q, ATT_Q)
        q = q_ref[pl.ds(r0, ATT_Q), :] * (SB_DIM ** -0.5)
        zero = jnp.zeros_like(q)
        qs_ref[0:ATT_Q, :] = jnp.where(first_q, q, zero)
        qs_ref[ATT_Q:, :] = jnp.where(first_q, zero, q)
        blank = jnp.zeros((CHUNK, CHUNK), F32)
        carry = [blank] * (2 * n_sub)
        acc = [blank] * n_sub

        jobs = []
        for jd in reversed(range(n_sub)):
            kb, v_heads = load_kv(t * n_sub + jd)
            for s in range(jd, n_sub):
                for head in range(2):
                    r = head * ATT_Q + s * CHUNK
                    jobs.append((_dot_nt(qs_ref[r:r + CHUNK, :], kb), s, head, s == jd, v_heads[head]))
        if not first:
            jobs += older_jobs(t, 0)
        run(jobs, carry, acc)

        def top_carry(carry):
            top = carry[0]
            for other in carry[1:]:
                top = jnp.maximum(top, other)
            return jnp.max(top)

        def older(state):
            trip, _, carry, acc = state
            carry, acc = list(carry), list(acc)
            run(older_jobs(t, trip), carry, acc)
            return trip + 1, top_carry(carry), tuple(carry), tuple(acc)

        def unfinished(state):
            return (state[0] < (t * n_sub) // ATT_KV_STEP) & (state[1] > EXP_UNDERFLOW)

        if not first:
            start = (jnp.int32(1), top_carry(carry), tuple(carry), tuple(acc))
            acc = lax.while_loop(unfinished, older, start)[3]
        for s in range(n_sub):
            o_ref[pl.ds(r0 + s * CHUNK, CHUNK), :] = acc[s].astype(o_ref.dtype)

    tile(0, True)

    def later_tile(t, c):
        tile(t, False)
        return c

    lax.fori_loop(1, seq // ATT_Q, later_tile, 0)


def _stick_breaking(qkv, batch, seq):
    pairs = qkv.shape[0] // 3
    n = qkv.shape[1]
    tri = (jnp.arange(CHUNK)[:, None] >= jnp.arange(CHUNK)[None, :])
    cum = jnp.concatenate([tri.astype(BF16), jnp.ones((CHUNK, CHUNK), BF16)], axis=1)
    cum = jnp.concatenate([cum, cum], axis=0)
    blk = (None, seq, LANES)
    return pl.pallas_call(
        _stick_kernel,
        grid=(batch, pairs),
        in_specs=[pl.BlockSpec(blk, lambda b, p: (p, b, 0)),
                  pl.BlockSpec(blk, lambda b, p: (pairs + p, b, 0)),
                  pl.BlockSpec(blk, lambda b, p: (2 * pairs + p, b, 0)),
                  pl.BlockSpec((2 * CHUNK, 2 * CHUNK), lambda b, p: (0, 0))],
        out_specs=pl.BlockSpec(blk, lambda b, p: (p, b, 0)),
        out_shape=jax.ShapeDtypeStruct((pairs, n, LANES), BF16),
        scratch_shapes=[pltpu.VMEM((2 * ATT_Q, LANES), BF16)],
        compiler_params=_params("parallel", "parallel"),
        name="stick_breaking",
    )(qkv, qkv, qkv, cum)


def _router_logits(xn, whi_ref, wlo_ref, b_ref):
    hi, lo = _split_bf16(xn)
    return _dot(hi, whi_ref[...]) + _dot(hi, wlo_ref[...]) + _dot(lo, whi_ref[...]) + b_ref[...]


def _router_kernel(h_ref, nw_ref, whi_ref, wlo_ref, b_ref, tri_ref, bucket_ref, rank_ref, counts_ref, run_ref):
    @pl.when(pl.program_id(0) == 0)
    def _():
        run_ref[...] = jnp.zeros_like(run_ref)

    logits = _router_logits(_rms(h_ref[...], nw_ref[...]), whi_ref, wlo_ref, b_ref)
    lane = lax.broadcasted_iota(I32, logits.shape, 1)
    neg = jnp.float32(-jnp.inf)

    def first_max(vals):
        m = jnp.max(vals, axis=-1, keepdims=True)
        return jnp.min(jnp.where(vals == m, lane, LANES), axis=-1, keepdims=True)

    group = first_max(jnp.where(lane < MOE_GROUPS, logits, neg))
    rel = lane - MOE_GROUPS
    in_group = (rel >= 0) & (rel < MOE_EXPERTS) & ((rel >> 3) == group)
    ev = jnp.where(in_group, logits, neg)
    i1 = first_max(ev)
    i2 = first_max(jnp.where(lane == i1, neg, ev))
    a = (i1 - MOE_GROUPS) & (MOE_PER_GROUP - 1)
    b = (i2 - MOE_GROUPS) & (MOE_PER_GROUP - 1)
    lo = jnp.minimum(a, b)
    hi = jnp.maximum(a, b)
    pair = (lo * (2 * MOE_PER_GROUP - 1 - lo)) // 2 + (hi - lo - 1)
    bucket = group * MOE_PAIRS + pair

    onehot = lane == bucket
    before = _dot(tri_ref[...], onehot.astype(BF16)) + run_ref[...]
    rank = jnp.sum(jnp.where(onehot, before, 0.0), axis=-1, keepdims=True)
    run_ref[...] += jnp.sum(onehot.astype(F32), axis=0, keepdims=True)
    bucket_ref[...] = bucket
    rank_ref[...] = rank.astype(I32)
    counts_ref[...] = run_ref[...]


def _router(h, norm_w, whi, wlo, bias):
    n, d = h.shape
    const = lambda i: (0, 0)
    tri = (jnp.arange(ROW_TILE)[:, None] > jnp.arange(ROW_TILE)[None, :]).astype(BF16)
    col = jax.ShapeDtypeStruct((n, 1), I32)
    return pl.pallas_call(
        _router_kernel,
        grid=(n // ROW_TILE,),
        in_specs=[pl.BlockSpec((ROW_TILE, d), lambda i: (i, 0)),
                  pl.BlockSpec((1, d), const),
                  pl.BlockSpec((d, LANES), const),
                  pl.BlockSpec((d, LANES), const),
                  pl.BlockSpec((1, LANES), const),
                  pl.BlockSpec((ROW_TILE, ROW_TILE), const)],
        out_specs=[pl.BlockSpec((ROW_TILE, 1), lambda i: (i, 0)),
                   pl.BlockSpec((ROW_TILE, 1), lambda i: (i, 0)),
                   pl.BlockSpec((1, LANES), const)],
        out_shape=[col, col, jax.ShapeDtypeStruct((1, LANES), F32)],
        scratch_shapes=[pltpu.VMEM((1, LANES), F32)],
        compiler_params=_params("arbitrary"),
        name="moe_router",
    )(h, norm_w.reshape(1, d), whi, wlo, bias, tri)


def _slot_kernel(bucket_ref, rank_ref, start_ref, slot_ref):
    lane = lax.broadcasted_iota(I32, (bucket_ref.shape[0], LANES), 1)
    start = jnp.sum(jnp.where(lane == bucket_ref[...], start_ref[...], 0), axis=-1, keepdims=True)
    slot_ref[...] = start + rank_ref[...]


def _slots(bucket, rank, start):
    n = bucket.shape[0]
    rows = ROW_TILE
    return pl.pallas_call(
        _slot_kernel,
        grid=(n // rows,),
        in_specs=[pl.BlockSpec((rows, 1), lambda i: (i, 0)),
                  pl.BlockSpec((rows, 1), lambda i: (i, 0)),
                  pl.BlockSpec((1, LANES), lambda i: (0, 0))],
        out_specs=pl.BlockSpec((rows, 1), lambda i: (i, 0)),
        out_shape=jax.ShapeDtypeStruct((n, 1), I32),
        compiler_params=_params("parallel"),
        name="moe_slots",
    )(bucket, rank, start)


def _tile_tables(counts, n):
    t = MOE_TILE
    n_tiles = n // t + MOE_BUCKETS
    counts = counts.reshape(LANES).astype(I32)[:MOE_BUCKETS]
    tiles = (counts + t - 1) // t
    tile_end = jnp.cumsum(tiles)
    tile_start = tile_end - tiles
    used = tile_end[-1]
    tile_id = jnp.arange(n_tiles, dtype=I32)
    block = jnp.minimum(tile_id, used - 1)
    tile_bucket = jnp.sum((tile_end[None, :] <= block[:, None]).astype(I32), axis=1)
    onehot = (tile_bucket[:, None] == jnp.arange(MOE_BUCKETS, dtype=I32)[None, :]).astype(I32)
    bucket_count = jnp.sum(onehot * counts[None, :], axis=1)
    bucket_start = jnp.sum(onehot * tile_start[None, :], axis=1)
    rows = jnp.clip(bucket_count - (block - bucket_start) * t, 0, t)
    rows = jnp.where(tile_id < used, rows, 0)
    group = tile_bucket // MOE_PAIRS
    pair = tile_bucket % MOE_PAIRS
    lo_tab, hi_tab = [], []
    for lo in range(MOE_PER_GROUP):
        for hi in range(lo + 1, MOE_PER_GROUP):
            lo_tab.append(lo)
            hi_tab.append(hi)
    pair_hot = (pair[:, None] == jnp.arange(MOE_PAIRS, dtype=I32)[None, :]).astype(I32)
    e_lo = group * MOE_PER_GROUP + jnp.sum(pair_hot * jnp.asarray(lo_tab, I32)[None, :], axis=1)
    e_hi = group * MOE_PER_GROUP + jnp.sum(pair_hot * jnp.asarray(hi_tab, I32)[None, :], axis=1)
    start = jnp.pad(tile_start * t, (0, LANES - MOE_BUCKETS)).reshape(1, LANES)
    return start, (block, rows.astype(I32), group, e_lo, e_hi)


def _dispatch_kernel(slot_ref, h_ref, init_hbm, out_hbm, sem):
    del init_hbm
    rows = h_ref.shape[0]
    base = pl.program_id(0) * rows

    for r in range(rows):
        pltpu.make_async_copy(h_ref.at[r], out_hbm.at[slot_ref[base + r]], sem).start()
    pltpu.make_async_copy(h_ref, out_hbm.at[pl.ds(0, rows)], sem).wait()


def _dispatch(h, slot, n_slots):
    n, d = h.shape
    grid_spec = pltpu.PrefetchScalarGridSpec(
        num_scalar_prefetch=1,
        grid=(n // ROW_TILE,),
        in_specs=[pl.BlockSpec((ROW_TILE, d), lambda i, slot: (i, 0)),
                  pl.BlockSpec(memory_space=pl.ANY)],
        out_specs=pl.BlockSpec(memory_space=pl.ANY),
        scratch_shapes=[pltpu.SemaphoreType.DMA(())],
    )
    return pl.pallas_call(
        _dispatch_kernel,
        grid_spec=grid_spec,
        out_shape=jax.ShapeDtypeStruct((n_slots, d), F32),
        input_output_aliases={2: 0},
        compiler_params=_params("arbitrary"),
        name="moe_dispatch",
    )(slot, h, jnp.zeros((n_slots, d), F32))


def _moe_kernel(block_ref, rows_ref, group_ref, elo_ref, ehi_ref,
                x_ref, nw_ref, wr_ref, b_ref, wg1_ref, wu1_ref, wd1_ref, wg2_ref, wu2_ref, wd2_ref,
                o_ref):
    i = pl.program_id(0)

    @pl.when(rows_ref[i] == 0)
    def _():
        o_ref[...] = jnp.zeros_like(o_ref)

    @pl.when(rows_ref[i] > 0)
    def _():
        x = x_ref[...]
        xb = _rms(x, nw_ref[...]).astype(BF16)
        logits = _dot(xb, wr_ref[...]) + b_ref[...]
        lane = lax.broadcasted_iota(I32, logits.shape, 1)

        def pick(idx):
            return jnp.sum(jnp.where(lane == idx, logits, 0.0), axis=-1, keepdims=True)

        lg = pick(group_ref[i])
        gexp = jnp.exp(jnp.where(lane < MOE_GROUPS, logits - lg, -jnp.inf))
        group_w = 1.0 / jnp.sum(gexp, axis=-1, keepdims=True)
        l1 = pick(MOE_GROUPS + elo_ref[i])
        l2 = pick(MOE_GROUPS + ehi_ref[i])
        m = jnp.maximum(l1, l2)
        e1 = jnp.exp(l1 - m)
        e2 = jnp.exp(l2 - m)
        w1 = e1 / (e1 + e2) * group_w
        w2 = e2 / (e1 + e2) * group_w

        hid1 = jax.nn.silu(_dot(xb, wg1_ref[...])) * _dot(xb, wu1_ref[...]) * w1
        hid2 = jax.nn.silu(_dot(xb, wg2_ref[...])) * _dot(xb, wu2_ref[...]) * w2
        o_ref[...] = x + _dot(hid1.astype(BF16), wd1_ref[...]) + _dot(hid2.astype(BF16), wd2_ref[...])


def _moe(xs, norm_w, w_route, bias, w_gate, w_up, w_down, tables):
    n_slots, d = xs.shape
    hidden = w_gate.shape[2]
    const = lambda i, *_: (0, 0)
    tile = lambda i, block, rows, group, elo, ehi: (block[i], 0)
    lo_w = lambda i, block, rows, group, elo, ehi: (elo[i], 0, 0)
    hi_w = lambda i, block, rows, group, elo, ehi: (ehi[i], 0, 0)
    grid_spec = pltpu.PrefetchScalarGridSpec(
        num_scalar_prefetch=5,
        grid=(n_slots // MOE_TILE,),
        in_specs=[pl.BlockSpec((MOE_TILE, d), tile),
                  pl.BlockSpec((1, d), const),
                  pl.BlockSpec((d, LANES), const),
                  pl.BlockSpec((1, LANES), const),
                  pl.BlockSpec((None, d, hidden), lo_w),
                  pl.BlockSpec((None, d, hidden), lo_w),
                  pl.BlockSpec((None, hidden, d), lo_w),
                  pl.BlockSpec((None, d, hidden), hi_w),
                  pl.BlockSpec((None, d, hidden), hi_w),
                  pl.BlockSpec((None, hidden, d), hi_w)],
        out_specs=pl.BlockSpec((MOE_TILE, d), lambda i, *_: (i, 0)),
    )
    return pl.pallas_call(
        _moe_kernel,
        grid_spec=grid_spec,
        out_shape=jax.ShapeDtypeStruct((n_slots, d), F32),
        compiler_params=_params("arbitrary"),
        name="sparse_moe",
    )(*tables, xs, norm_w.reshape(1, d), w_route, bias, w_gate, w_up, w_down, w_gate, w_up, w_down)


def _ple_kernel(slot_ref, ys_hbm, p_ref, nw_ref, wg_ref, wp_ref, fw_ref, o_ref, buf, sem, *, final):
    i = pl.program_id(0)
    last = pl.num_programs(0) - 1
    rows = o_ref.shape[0]
    which = i % 2

    def wait(b):
        pltpu.make_async_copy(ys_hbm.at[pl.ds(0, rows)], buf.at[b], sem.at[b]).wait()

    @pl.when(i == 0)
    def _():
        def issue(r, c):
            pltpu.make_async_copy(ys_hbm.at[slot_ref[r]], buf.at[0, r], sem.at[0]).start()
            return c

        lax.fori_loop(0, rows, issue, 0, unroll=DMA_UNROLL)

    wait(which)
    base = jnp.minimum(i + 1, last) * rows
    for r in range(rows):
        pltpu.make_async_copy(ys_hbm.at[slot_ref[base + r]], buf.at[1 - which, r], sem.at[1 - which]).start()
    x = buf[which]
    gate = jax.nn.sigmoid(_dot(_rms(x, nw_ref[...]).astype(BF16), wg_ref[...]))
    out = x + gate * _dot(p_ref[...].astype(BF16), wp_ref[...])
    if final:
        out = _rms(out, fw_ref[...])
    o_ref[...] = out

    @pl.when(i == last)
    def _():
        wait(1 - which)


def _ple(ys, slot, p, norm_w, w_gate, w_proj, final_w, final):
    n = p.shape[0]
    d = ys.shape[1]
    const = lambda i, slot: (0, 0)
    grid_spec = pltpu.PrefetchScalarGridSpec(
        num_scalar_prefetch=1,
        grid=(n // ROW_TILE,),
        in_specs=[pl.BlockSpec(memory_space=pl.ANY),
                  pl.BlockSpec((ROW_TILE, p.shape[1]), lambda i, slot: (i, 0)),
                  pl.BlockSpec((1, d), const),
                  pl.BlockSpec(w_gate.shape, const),
                  pl.BlockSpec(w_proj.shape, const),
                  pl.BlockSpec((1, d), const)],
        out_specs=pl.BlockSpec((ROW_TILE, d), lambda i, slot: (i, 0)),
        scratch_shapes=[pltpu.VMEM((2, ROW_TILE, d), F32),
                        pltpu.SemaphoreType.DMA((2,))],
    )
    return pl.pallas_call(
        functools.partial(_ple_kernel, final=final),
        grid_spec=grid_spec,
        out_shape=jax.ShapeDtypeStruct((n, d), F32),
        compiler_params=_params("arbitrary"),
        name="per_layer_embedding",
    )(slot, ys, p, norm_w.reshape(1, d), w_gate, w_proj, final_w.reshape(1, d))


def kernel(x, p, attn_norm_w, ffn_norm_w, final_norm_w, even_w_in, even_w_out, ret_norm_w, sg_norm_w, sg_spatial_w, sg_spatial_b, odd_w_in, odd_w_out, moe_w_group, moe_b_group, moe_w_expert, moe_b_expert, moe_w_gate, moe_w_up, moe_w_down, ple_norm_w, ple_w_gate, ple_w_proj):
    batch, seq, d = x.shape
    depth = p.shape[0]
    n = batch * seq
    n_slots = n + MOE_BUCKETS * MOE_TILE
    h = x.reshape(n, d)
    for i in range(depth):
        j = i // 2
        if i % 2 == 0:
            proj = _norm_matmul(h, attn_norm_w[i], even_w_in[j].astype(BF16), split=False)
            h = _even_mixer(h, proj, batch, seq, even_w_out[j].astype(BF16), ret_norm_w[j], sg_norm_w[j],
                            sg_spatial_w[j], sg_spatial_b[j])
        else:
            qkv = _norm_matmul(h, attn_norm_w[i], odd_w_in[j].astype(BF16), split=True)
            att = _stick_breaking(qkv, batch, seq)
            h = _matmul_residual(att, odd_w_out[j].astype(BF16), h)

        w_route = jnp.concatenate([moe_w_group[i], moe_w_expert[i]], axis=1)
        w_route = jnp.pad(w_route, ((0, 0), (0, LANES - w_route.shape[1])))
        whi, wlo = _split_bf16(w_route)
        bias = jnp.concatenate([moe_b_group[i], moe_b_expert[i]])
        bias = jnp.pad(bias, (0, LANES - bias.shape[0])).reshape(1, LANES)
        bucket, rank, counts = _router(h, ffn_norm_w[i], whi, wlo, bias)
        start, tables = _tile_tables(counts, n)
        slot = _slots(bucket, rank, start).reshape(n)
        xs = _dispatch(h, slot, n_slots)
        ys = _moe(xs, ffn_norm_w[i], whi, bias, moe_w_gate[i].astype(BF16), moe_w_up[i].astype(BF16),
                  moe_w_down[i].astype(BF16), tables)
        h = _ple(ys, slot, p[i].reshape(n, -1), ple_norm_w[i], ple_w_gate[i].astype(BF16),
                 ple_w_proj[i].astype(BF16), final_norm_w, final=(i == depth - 1))
    return h.reshape(batch, seq, d)
```

```python
import functools

import jax
import jax.numpy as jnp
from jax import lax
from jax.experimental import pallas as pl
from jax.experimental.pallas import tpu as pltpu

F32 = jnp.float32
BF16 = jnp.bfloat16
I32 = jnp.int32

LANES = 128
CHUNK = 128
RET_HEADS = 4
RET_DIM = 128
RET_WIDTH = RET_HEADS * RET_DIM
SG_GROUPS = 4
SG_DIM = 128
SG_WIDTH = SG_GROUPS * SG_DIM
SB_DIM = 64
MOE_GROUPS = 4
MOE_PER_GROUP = 8
MOE_EXPERTS = MOE_GROUPS * MOE_PER_GROUP
MOE_PAIRS = MOE_PER_GROUP * (MOE_PER_GROUP - 1) // 2
MOE_BUCKETS = MOE_GROUPS * MOE_PAIRS
ROPE_BASE = 10000.0
EPS = 1e-6

ROW_TILE = 512
MIX_TILE = 512
PROJ_CHUNK = 512
MOE_TILE = 256
ATT_Q = 256
ATT_KV_STEP = 2
EXP_UNDERFLOW = -104.0
DMA_UNROLL = 8
DMA_THREADS = 2
VMEM_LIMIT = 56 * 1024 * 1024


def _params(*sem):
    return pltpu.CompilerParams(dimension_semantics=sem, vmem_limit_bytes=VMEM_LIMIT)


def _rms(x, w):
    return x * lax.rsqrt(jnp.mean(x * x, axis=-1, keepdims=True) + EPS) * w


def _dot(a, b):
    return jnp.dot(a, b, preferred_element_type=F32)


def _dot_nt(a, b):
    return lax.dot_general(a, b, (((1,), (1,)), ((), ())), preferred_element_type=F32)


def _dot_tn(a, b):
    return lax.dot_general(a, b, (((0,), (0,)), ((), ())), preferred_element_type=F32)


def _aligned(index, multiple):
    return index if isinstance(index, int) else pl.multiple_of(index, multiple)


def _split_bf16(x):
    hi = x.astype(BF16)
    lo = (x - hi.astype(F32)).astype(BF16)
    return hi, lo


def _project(x, nw_ref, w_ref, o_ref):
    xn = _rms(x, nw_ref[...]).astype(BF16)
    for c in range(0, w_ref.shape[1], PROJ_CHUNK):
        res = _dot(xn, w_ref[:, c:c + PROJ_CHUNK]).astype(o_ref.dtype)
        if len(o_ref.shape) == 3:
            for s in range(PROJ_CHUNK // LANES):
                o_ref[c // LANES + s] = res[:, s * LANES:(s + 1) * LANES]
        else:
            o_ref[:, c:c + PROJ_CHUNK] = res


def _proj_out(n, m, split, index):
    if split:
        return (pl.BlockSpec((m // LANES, ROW_TILE, LANES), lambda *a: (0, index(*a), 0)),
                jax.ShapeDtypeStruct((m // LANES, n, LANES), BF16))
    return (pl.BlockSpec((ROW_TILE, m), lambda *a: (index(*a), 0)), jax.ShapeDtypeStruct((n, m), BF16))


def _norm_matmul_kernel(h_ref, nw_ref, w_ref, o_ref):
    _project(h_ref[...], nw_ref, w_ref, o_ref)


def _norm_matmul(h, norm_w, w, split):
    n, d = h.shape
    out_spec, out_shape = _proj_out(n, w.shape[1], split, lambda i: i)
    return pl.pallas_call(
        _norm_matmul_kernel,
        grid=(n // ROW_TILE,),
        in_specs=[pl.BlockSpec((ROW_TILE, d), lambda i: (i, 0)),
                  pl.BlockSpec((1, d), lambda i: (0, 0)),
                  pl.BlockSpec(w.shape, lambda i: (0, 0))],
        out_specs=out_spec,
        out_shape=out_shape,
        compiler_params=_params("parallel"),
        name="norm_matmul",
    )(h, norm_w.reshape(1, d), w)


def _matmul_residual_kernel(a_ref, w_ref, h_ref, o_ref):
    a = jnp.concatenate([a_ref[c] for c in range(a_ref.shape[0])], axis=1)
    o_ref[...] = h_ref[...] + _dot(a, w_ref[...])


def _matmul_residual(a, w, h):
    groups, n, _ = a.shape
    d = w.shape[1]
    return pl.pallas_call(
        _matmul_residual_kernel,
        grid=(n // ROW_TILE,),
        in_specs=[pl.BlockSpec((groups, ROW_TILE, LANES), lambda i: (0, i, 0)),
                  pl.BlockSpec(w.shape, lambda i: (0, 0)),
                  pl.BlockSpec((ROW_TILE, d), lambda i: (i, 0))],
        out_specs=pl.BlockSpec((ROW_TILE, d), lambda i: (i, 0)),
        out_shape=jax.ShapeDtypeStruct((n, d), F32),
        compiler_params=_params("parallel"),
        name="matmul_residual",
    )(a, w, h)


def _even_mixer_kernel(h_ref, proj_ref, cos_ref, sin_ref, intra_ref, qdec_ref, kdec_ref, cdec_ref,
                       retw_ref, sgnw_ref, sgw_ref, sgb_ref, wout_ref, o_ref, state_ref, mixed_ref):
    @pl.when(pl.program_id(1) == 0)
    def _():
        state_ref[...] = jnp.zeros_like(state_ref)

    def chunk(c, carry):
        r0 = pl.multiple_of(c * CHUNK, CHUNK)
        rows = pl.ds(r0, CHUNK)
        cos = cos_ref[rows, :]
        sin = sin_ref[rows, :]

        def rope(t):
            return t * cos + pltpu.roll(t, RET_DIM // 2, 1) * sin

        heads = range(RET_HEADS)
        lanes = [slice(hd * RET_DIM, (hd + 1) * RET_DIM) for hd in heads]

        def field(idx, hd):
            return proj_ref[rows, idx * RET_WIDTH + hd * RET_DIM:idx * RET_WIDTH + (hd + 1) * RET_DIM]

        q = [rope(field(0, hd).astype(F32)) for hd in heads]
        k = [rope(field(1, hd).astype(F32)) for hd in heads]
        v = [field(2, hd) for hd in heads]
        scores = [_dot_nt(q[hd].astype(BF16), k[hd].astype(BF16)) * intra_ref[hd] for hd in heads]
        st = [state_ref[hd] for hd in heads]
        out = [_dot(scores[hd].astype(BF16), v[hd]) + _dot((q[hd] * qdec_ref[hd]).astype(BF16), st[hd].astype(BF16))
               for hd in heads]
        for hd in heads:
            state_ref[hd] = st[hd] * cdec_ref[hd] + _dot_tn((k[hd] * kdec_ref[hd]).astype(BF16), v[hd])
        for hd in heads:
            mu = jnp.mean(out[hd], axis=-1, keepdims=True)
            cen = out[hd] - mu
            var = jnp.mean(cen * cen, axis=-1, keepdims=True)
            ret = cen * lax.rsqrt(var + EPS) * retw_ref[:, lanes[hd]]
            mixed_ref[rows, lanes[hd]] = (jax.nn.silu(field(3, hd).astype(F32)) * ret).astype(BF16)

        base = 4 * RET_WIDTH
        gv = jax.nn.gelu(proj_ref[rows, base + SG_WIDTH:base + 2 * SG_WIDTH].astype(F32))
        vsn = _rms(gv, sgnw_ref[...]).astype(BF16)
        for gi in range(SG_GROUPS):
            lanes = slice(gi * SG_DIM, (gi + 1) * SG_DIM)
            u = proj_ref[rows, base + gi * SG_DIM:base + (gi + 1) * SG_DIM].astype(F32)
            mix = _dot(sgw_ref[gi], vsn[:, lanes]) + sgb_ref[gi]
            mixed_ref[rows, RET_WIDTH + gi * SG_DIM:RET_WIDTH + (gi + 1) * SG_DIM] = (
                jax.nn.gelu(u) * mix).astype(BF16)
        return carry

    lax.fori_loop(0, h_ref.shape[0] // CHUNK, chunk, 0)
    o_ref[...] = h_ref[...] + _dot(mixed_ref[...], wout_ref[...])


def _even_mixer(h, proj, batch, seq, w_out, ret_norm_w, sg_norm_w, sg_w, sg_b):
    n, d = h.shape
    steps = seq // MIX_TILE
    inv_freq = ROPE_BASE ** (-jnp.arange(0, RET_DIM, 2, dtype=F32) / RET_DIM)
    ang = jnp.arange(seq, dtype=F32)[:, None] * inv_freq[None, :]
    cos = jnp.concatenate([jnp.cos(ang), jnp.cos(ang)], axis=-1)
    sin = jnp.concatenate([-jnp.sin(ang), jnp.sin(ang)], axis=-1)
    log_gamma = jnp.log1p(-jnp.exp2(-5.0 - jnp.arange(RET_HEADS, dtype=F32)))
    pos = jnp.arange(CHUNK, dtype=F32)
    diff = pos[:, None] - pos[None, :]
    scale = RET_DIM ** -0.5
    intra = jnp.where(diff >= 0, jnp.exp(log_gamma[:, None, None] * jnp.maximum(diff, 0.0)), 0.0) * scale
    bshape = (RET_HEADS, CHUNK, RET_DIM)
    qdec = jnp.broadcast_to(jnp.exp(log_gamma[:, None] * (pos[None, :] + 1.0))[:, :, None], bshape)
    kdec = jnp.broadcast_to(jnp.exp(log_gamma[:, None] * (CHUNK - 1.0 - pos[None, :]))[:, :, None] * scale, bshape)
    cdec = jnp.broadcast_to(jnp.exp(log_gamma * CHUNK)[:, None, None], bshape)
    tril = jnp.tril(jnp.ones((CHUNK, CHUNK), dtype=bool))
    sgw = jnp.where(tril, sg_w, jnp.zeros_like(sg_w)).astype(BF16)
    sgb = jnp.broadcast_to(sg_b[:, :, None], (SG_GROUPS, CHUNK, SG_DIM))

    const3 = lambda b, s: (0, 0, 0)
    const2 = lambda b, s: (0, 0)
    row = lambda b, s: (b * steps + s, 0)
    return pl.pallas_call(
        _even_mixer_kernel,
        grid=(batch, steps),
        in_specs=[pl.BlockSpec((MIX_TILE, d), row),
                  pl.BlockSpec((MIX_TILE, proj.shape[1]), row),
                  pl.BlockSpec((MIX_TILE, RET_DIM), lambda b, s: (s, 0)),
                  pl.BlockSpec((MIX_TILE, RET_DIM), lambda b, s: (s, 0)),
                  pl.BlockSpec(bshape, const3),
                  pl.BlockSpec(bshape, const3),
                  pl.BlockSpec(bshape, const3),
                  pl.BlockSpec(bshape, const3),
                  pl.BlockSpec((1, RET_WIDTH), const2),
                  pl.BlockSpec((1, SG_WIDTH), const2),
                  pl.BlockSpec((SG_GROUPS, CHUNK, CHUNK), const3),
                  pl.BlockSpec((SG_GROUPS, CHUNK, SG_DIM), const3),
                  pl.BlockSpec(w_out.shape, const2)],
        out_specs=pl.BlockSpec((MIX_TILE, d), row),
        out_shape=jax.ShapeDtypeStruct((n, d), F32),
        scratch_shapes=[pltpu.VMEM((RET_HEADS, RET_DIM, RET_DIM), F32),
                        pltpu.VMEM((MIX_TILE, RET_WIDTH + SG_WIDTH), BF16)],
        compiler_params=_params("parallel", "arbitrary"),
        name="even_mixer",
    )(h, proj, cos, sin, intra, qdec, kdec, cdec, ret_norm_w.reshape(1, -1), sg_norm_w.reshape(1, -1),
      sgw, sgb, w_out)


def _stick_kernel(q_ref, k_ref, v_ref, cum_ref, o_ref, qs_ref):
    seq = q_ref.shape[0]
    n_sub = ATT_Q // CHUNK
    row = lax.broadcasted_iota(I32, (CHUNK, CHUNK), 0)
    lane = lax.broadcasted_iota(I32, (CHUNK, CHUNK), 1)
    tri = lane < row
    first_k = lane < SB_DIM
    first_q = lax.broadcasted_iota(I32, (ATT_Q, LANES), 1) < SB_DIM
    cum = cum_ref[...]

    def run(jobs, carry, acc):
        stacked = []
        for z, _, _, diagonal, _ in jobs:
            stay = jnp.minimum(-z, 0.0) - jnp.log(1.0 + jnp.exp(-jnp.abs(z)))
            if diagonal:
                stay = jnp.where(tri, stay, 0.0)
            stacked.append(stay.astype(BF16))
        sums = [_dot(lhs, cum) for lhs in stacked]
        weights = []
        for (z, s, head, diagonal, _), both in zip(jobs, sums):
            c = head * n_sub + s
            a = jnp.exp(z + carry[c] + both[:, :CHUNK])
            if diagonal:
                a = jnp.where(tri, a, 0.0)
            carry[c] = carry[c] + both[:, CHUNK:]
            weights.append(a.astype(BF16))
        for (_, s, _, _, v_head), a in zip(jobs, weights):
            acc[s] = acc[s] + _dot(a, v_head)

    def load_kv(j):
        keys = pl.ds(_aligned(j * CHUNK, CHUNK), CHUNK)
        kb = k_ref[keys, :]
        vb = v_ref[keys, :]
        zero = jnp.zeros_like(vb)
        return kb, (jnp.where(first_k, vb, zero), jnp.where(first_k, zero, vb))

    def older_jobs(t, trip):
        jobs = []
        for b in range(ATT_KV_STEP):
            kb, v_heads = load_kv(t * n_sub - 1 - trip * ATT_KV_STEP - b)
            z_all = _dot_nt(qs_ref[...], kb)
            for s in range(n_sub):
                for head in range(2):
                    r = head * ATT_Q + s * CHUNK
                    jobs.append((z_all[r:r + CHUNK, :], s, head, False, v_heads[head]))
        return jobs

    def tile(t, first):
        r0 = _aligned(t * ATT_Q, ATT_Q)
        q = q_ref[pl.ds(r0, ATT_Q), :] * (SB_DIM ** -0.5)
        zero = jnp.zeros_like(q)
        qs_ref[0:ATT_Q, :] = jnp.where(first_q, q, zero)
        qs_ref[ATT_Q:, :] = jnp.where(first_q, zero, q)
        blank = jnp.zeros((CHUNK, CHUNK), F32)
        carry = [blank] * (2 * n_sub)
        acc = [blank] * n_sub

        jobs = []
        for jd in reversed(range(n_sub)):
            kb, v_heads = load_kv(t * n_sub + jd)
            for s in range(jd, n_sub):
                for head in range(2):
                    r = head * ATT_Q + s * CHUNK
                    jobs.append((_dot_nt(qs_ref[r:r + CHUNK, :], kb), s, head, s == jd, v_heads[head]))
        if not first:
            jobs += older_jobs(t, 0)
        run(jobs, carry, acc)

        def top_carry(carry):
            top = carry[0]
            for other in carry[1:]:
                top = jnp.maximum(top, other)
            return jnp.max(top)

        def older(state):
            trip, _, carry, acc = state
            carry, acc = list(carry), list(acc)
            run(older_jobs(t, trip), carry, acc)
            return trip + 1, top_carry(carry), tuple(carry), tuple(acc)

        def unfinished(state):
            return (state[0] < (t * n_sub) // ATT_KV_STEP) & (state[1] > EXP_UNDERFLOW)

        if not first:
            start = (jnp.int32(1), top_carry(carry), tuple(carry), tuple(acc))
            acc = lax.while_loop(unfinished, older, start)[3]
        for s in range(n_sub):
            o_ref[pl.ds(r0 + s * CHUNK, CHUNK), :] = acc[s].astype(o_ref.dtype)

    tile(0, True)

    def later_tile(t, c):
        tile(t, False)
        return c

    lax.fori_loop(1, seq // ATT_Q, later_tile, 0)


def _stick_breaking(qkv, batch, seq):
    pairs = qkv.shape[0] // 3
    n = qkv.shape[1]
    tri = (jnp.arange(CHUNK)[:, None] >= jnp.arange(CHUNK)[None, :])
    cum = jnp.concatenate([tri.astype(BF16), jnp.ones((CHUNK, CHUNK), BF16)], axis=1)
    blk = (None, seq, LANES)
    return pl.pallas_call(
        _stick_kernel,
        grid=(batch, pairs),
        in_specs=[pl.BlockSpec(blk, lambda b, p: (p, b, 0)),
                  pl.BlockSpec(blk, lambda b, p: (pairs + p, b, 0)),
                  pl.BlockSpec(blk, lambda b, p: (2 * pairs + p, b, 0)),
                  pl.BlockSpec((CHUNK, 2 * CHUNK), lambda b, p: (0, 0))],
        out_specs=pl.BlockSpec(blk, lambda b, p: (p, b, 0)),
        out_shape=jax.ShapeDtypeStruct((pairs, n, LANES), BF16),
        scratch_shapes=[pltpu.VMEM((2 * ATT_Q, LANES), BF16)],
        compiler_params=_params("parallel", "parallel"),
        name="stick_breaking",
    )(qkv, qkv, qkv, cum)


def _router_logits(xn, whi_ref, wlo_ref, b_ref):
    hi, lo = _split_bf16(xn)
    return _dot(hi, whi_ref[...]) + _dot(hi, wlo_ref[...]) + _dot(lo, whi_ref[...]) + b_ref[...]


def _router_kernel(h_ref, nw_ref, whi_ref, wlo_ref, b_ref, tri_ref, bucket_ref, rank_ref, counts_ref, run_ref):
    @pl.when(pl.program_id(0) == 0)
    def _():
        run_ref[...] = jnp.zeros_like(run_ref)

    logits = _router_logits(_rms(h_ref[...], nw_ref[...]), whi_ref, wlo_ref, b_ref)
    lane = lax.broadcasted_iota(I32, logits.shape, 1)
    neg = jnp.float32(-jnp.inf)

    def first_max(vals):
        m = jnp.max(vals, axis=-1, keepdims=True)
        return jnp.min(jnp.where(vals == m, lane, LANES), axis=-1, keepdims=True)

    group = first_max(jnp.where(lane < MOE_GROUPS, logits, neg))
    rel = lane - MOE_GROUPS
    in_group = (rel >= 0) & (rel < MOE_EXPERTS) & ((rel >> 3) == group)
    ev = jnp.where(in_group, logits, neg)
    i1 = first_max(ev)
    i2 = first_max(jnp.where(lane == i1, neg, ev))
    a = (i1 - MOE_GROUPS) & (MOE_PER_GROUP - 1)
    b = (i2 - MOE_GROUPS) & (MOE_PER_GROUP - 1)
    lo = jnp.minimum(a, b)
    hi = jnp.maximum(a, b)
    pair = (lo * (2 * MOE_PER_GROUP - 1 - lo)) // 2 + (hi - lo - 1)
    bucket = group * MOE_PAIRS + pair

    onehot = lane == bucket
    before = _dot(tri_ref[...], onehot.astype(BF16)) + run_ref[...]
    rank = jnp.sum(jnp.where(onehot, before, 0.0), axis=-1, keepdims=True)
    run_ref[...] += jnp.sum(onehot.astype(F32), axis=0, keepdims=True)
    bucket_ref[...] = bucket
    rank_ref[...] = rank.astype(I32)
    counts_ref[...] = run_ref[...]


def _router(h, norm_w, whi, wlo, bias):
    n, d = h.shape
    const = lambda i: (0, 0)
    tri = (jnp.arange(ROW_TILE)[:, None] > jnp.arange(ROW_TILE)[None, :]).astype(BF16)
    col = jax.ShapeDtypeStruct((n, 1), I32)
    return pl.pallas_call(
        _router_kernel,
        grid=(n // ROW_TILE,),
        in_specs=[pl.BlockSpec((ROW_TILE, d), lambda i: (i, 0)),
                  pl.BlockSpec((1, d), const),
                  pl.BlockSpec((d, LANES), const),
                  pl.BlockSpec((d, LANES), const),
                  pl.BlockSpec((1, LANES), const),
                  pl.BlockSpec((ROW_TILE, ROW_TILE), const)],
        out_specs=[pl.BlockSpec((ROW_TILE, 1), lambda i: (i, 0)),
                   pl.BlockSpec((ROW_TILE, 1), lambda i: (i, 0)),
                   pl.BlockSpec((1, LANES), const)],
        out_shape=[col, col, jax.ShapeDtypeStruct((1, LANES), F32)],
        scratch_shapes=[pltpu.VMEM((1, LANES), F32)],
        compiler_params=_params("arbitrary"),
        name="moe_router",
    )(h, norm_w.reshape(1, d), whi, wlo, bias, tri)


def _slot_kernel(bucket_ref, rank_ref, start_ref, slot_ref):
    lane = lax.broadcasted_iota(I32, (bucket_ref.shape[0], LANES), 1)
    start = jnp.sum(jnp.where(lane == bucket_ref[...], start_ref[...], 0), axis=-1, keepdims=True)
    slot_ref[...] = start + rank_ref[...]


def _slots(bucket, rank, start):
    n = bucket.shape[0]
    rows = min(n, 8 * ROW_TILE)
    return pl.pallas_call(
        _slot_kernel,
        grid=(n // rows,),
        in_specs=[pl.BlockSpec((rows, 1), lambda i: (i, 0)),
                  pl.BlockSpec((rows, 1), lambda i: (i, 0)),
                  pl.BlockSpec((1, LANES), lambda i: (0, 0))],
        out_specs=pl.BlockSpec((rows, 1), lambda i: (i, 0)),
        out_shape=jax.ShapeDtypeStruct((n, 1), I32),
        compiler_params=_params("parallel"),
        name="moe_slots",
    )(bucket, rank, start)


def _tile_tables(counts, n):
    t = MOE_TILE
    n_tiles = n // t + MOE_BUCKETS
    counts = counts.reshape(LANES).astype(I32)[:MOE_BUCKETS]
    tiles = (counts + t - 1) // t
    tile_end = jnp.cumsum(tiles)
    tile_start = tile_end - tiles
    used = tile_end[-1]
    tile_id = jnp.arange(n_tiles, dtype=I32)
    block = jnp.minimum(tile_id, used - 1)
    tile_bucket = jnp.sum((tile_end[None, :] <= block[:, None]).astype(I32), axis=1)
    onehot = (tile_bucket[:, None] == jnp.arange(MOE_BUCKETS, dtype=I32)[None, :]).astype(I32)
    bucket_count = jnp.sum(onehot * counts[None, :], axis=1)
    bucket_start = jnp.sum(onehot * tile_start[None, :], axis=1)
    rows = jnp.clip(bucket_count - (block - bucket_start) * t, 0, t)
    rows = jnp.where(tile_id < used, rows, 0)
    group = tile_bucket // MOE_PAIRS
    pair = tile_bucket % MOE_PAIRS
    lo_tab, hi_tab = [], []
    for lo in range(MOE_PER_GROUP):
        for hi in range(lo + 1, MOE_PER_GROUP):
            lo_tab.append(lo)
            hi_tab.append(hi)
    pair_hot = (pair[:, None] == jnp.arange(MOE_PAIRS, dtype=I32)[None, :]).astype(I32)
    e_lo = group * MOE_PER_GROUP + jnp.sum(pair_hot * jnp.asarray(lo_tab, I32)[None, :], axis=1)
    e_hi = group * MOE_PER_GROUP + jnp.sum(pair_hot * jnp.asarray(hi_tab, I32)[None, :], axis=1)
    start = jnp.pad(tile_start * t, (0, LANES - MOE_BUCKETS)).reshape(1, LANES)
    return start, (block, rows.astype(I32), group, e_lo, e_hi)


def _dispatch_kernel(slot_ref, h_ref, init_hbm, out_hbm, sem):
    del init_hbm
    rows = h_ref.shape[0]
    base = pl.program_id(0) * rows

    for r in range(rows):
        pltpu.make_async_copy(h_ref.at[r], out_hbm.at[slot_ref[base + r]], sem).start(priority=r % DMA_THREADS)
    pltpu.make_async_copy(h_ref, out_hbm.at[pl.ds(0, rows)], sem).wait()


def _dispatch(h, slot, init):
    n, d = h.shape
    grid_spec = pltpu.PrefetchScalarGridSpec(
        num_scalar_prefetch=1,
        grid=(n // ROW_TILE,),
        in_specs=[pl.BlockSpec((ROW_TILE, d), lambda i, slot: (i, 0)),
                  pl.BlockSpec(memory_space=pl.ANY)],
        out_specs=pl.BlockSpec(memory_space=pl.ANY),
        scratch_shapes=[pltpu.SemaphoreType.DMA(())],
    )
    return pl.pallas_call(
        _dispatch_kernel,
        grid_spec=grid_spec,
        out_shape=jax.ShapeDtypeStruct(init.shape, F32),
        input_output_aliases={2: 0},
        compiler_params=_params("arbitrary"),
        name="moe_dispatch",
    )(slot, h, init)


def _moe_kernel(block_ref, rows_ref, group_ref, elo_ref, ehi_ref,
                x_ref, nw_ref, wr_ref, b_ref, wg1_ref, wu1_ref, wd1_ref, wg2_ref, wu2_ref, wd2_ref,
                o_ref):
    i = pl.program_id(0)

    @pl.when(rows_ref[i] == 0)
    def _():
        o_ref[...] = jnp.zeros_like(o_ref)

    @pl.when(rows_ref[i] > 0)
    def _():
        x = x_ref[...]
        xb = _rms(x, nw_ref[...]).astype(BF16)
        logits = _dot(xb, wr_ref[...]) + b_ref[...]
        lane = lax.broadcasted_iota(I32, logits.shape, 1)

        def pick(idx):
            return jnp.sum(jnp.where(lane == idx, logits, 0.0), axis=-1, keepdims=True)

        lg = pick(group_ref[i])
        gexp = jnp.exp(jnp.where(lane < MOE_GROUPS, logits - lg, -jnp.inf))
        group_w = 1.0 / jnp.sum(gexp, axis=-1, keepdims=True)
        l1 = pick(MOE_GROUPS + elo_ref[i])
        l2 = pick(MOE_GROUPS + ehi_ref[i])
        m = jnp.maximum(l1, l2)
        e1 = jnp.exp(l1 - m)
        e2 = jnp.exp(l2 - m)
        w1 = e1 / (e1 + e2) * group_w
        w2 = e2 / (e1 + e2) * group_w

        hid1 = jax.nn.silu(_dot(xb, wg1_ref[...])) * _dot(xb, wu1_ref[...]) * w1
        hid2 = jax.nn.silu(_dot(xb, wg2_ref[...])) * _dot(xb, wu2_ref[...]) * w2
        o_ref[...] = x + _dot(hid1.astype(BF16), wd1_ref[...]) + _dot(hid2.astype(BF16), wd2_ref[...])


def _moe(xs, norm_w, w_route, bias, w_gate, w_up, w_down, tables):
    n_slots, d = xs.shape
    hidden = w_gate.shape[2]
    const = lambda i, *_: (0, 0)
    tile = lambda i, block, rows, group, elo, ehi: (block[i], 0)
    lo_w = lambda i, block, rows, group, elo, ehi: (elo[i], 0, 0)
    hi_w = lambda i, block, rows, group, elo, ehi: (ehi[i], 0, 0)
    grid_spec = pltpu.PrefetchScalarGridSpec(
        num_scalar_prefetch=5,
        grid=(n_slots // MOE_TILE,),
        in_specs=[pl.BlockSpec((MOE_TILE, d), tile),
                  pl.BlockSpec((1, d), const),
                  pl.BlockSpec((d, LANES), const),
                  pl.BlockSpec((1, LANES), const),
                  pl.BlockSpec((None, d, hidden), lo_w),
                  pl.BlockSpec((None, d, hidden), lo_w),
                  pl.BlockSpec((None, hidden, d), lo_w),
                  pl.BlockSpec((None, d, hidden), hi_w),
                  pl.BlockSpec((None, d, hidden), hi_w),
                  pl.BlockSpec((None, hidden, d), hi_w)],
        out_specs=pl.BlockSpec((MOE_TILE, d), lambda i, *_: (i, 0)),
    )
    return pl.pallas_call(
        _moe_kernel,
        grid_spec=grid_spec,
        out_shape=jax.ShapeDtypeStruct((n_slots, d), F32),
        compiler_params=_params("arbitrary"),
        name="sparse_moe",
    )(*tables, xs, norm_w.reshape(1, d), w_route, bias, w_gate, w_up, w_down, w_gate, w_up, w_down)


def _ple_kernel(slot_ref, ys_hbm, p_ref, nw_ref, wg_ref, wp_ref, post_ref, *rest, final):
    if final:
        o_ref, buf, sem = rest
    else:
        win_ref, o_ref, proj_ref, buf, sem = rest
    i = pl.program_id(0)
    last = pl.num_programs(0) - 1
    rows = o_ref.shape[0]
    which = i % 2

    def wait(b):
        pltpu.make_async_copy(ys_hbm.at[pl.ds(0, rows)], buf.at[b], sem.at[b]).wait()

    @pl.when(i == 0)
    def _():
        def issue(r, c):
            pltpu.make_async_copy(ys_hbm.at[slot_ref[r]], buf.at[0, r], sem.at[0]).start()
            return c

        lax.fori_loop(0, rows, issue, 0, unroll=DMA_UNROLL)

    wait(which)
    base = jnp.minimum(i + 1, last) * rows
    for r in range(rows):
        pltpu.make_async_copy(ys_hbm.at[slot_ref[base + r]], buf.at[1 - which, r],
                              sem.at[1 - which]).start(priority=r % DMA_THREADS)
    x = buf[which]
    gate = jax.nn.sigmoid(_dot(_rms(x, nw_ref[...]).astype(BF16), wg_ref[...]))
    out = x + gate * _dot(p_ref[...].astype(BF16), wp_ref[...])
    if final:
        o_ref[...] = _rms(out, post_ref[...])
    else:
        o_ref[...] = out
        _project(out, post_ref, win_ref, proj_ref)

    @pl.when(i == last)
    def _():
        wait(1 - which)


def _ple(ys, slot, p, norm_w, w_gate, w_proj, post_w, next_w_in=None, split=False):
    n = p.shape[0]
    d = ys.shape[1]
    final = next_w_in is None
    const = lambda i, slot: (0, 0)
    in_specs = [pl.BlockSpec(memory_space=pl.ANY),
                pl.BlockSpec((ROW_TILE, p.shape[1]), lambda i, slot: (i, 0)),
                pl.BlockSpec((1, d), const),
                pl.BlockSpec(w_gate.shape, const),
                pl.BlockSpec(w_proj.shape, const),
                pl.BlockSpec((1, d), const)]
    out_specs = pl.BlockSpec((ROW_TILE, d), lambda i, slot: (i, 0))
    out_shape = jax.ShapeDtypeStruct((n, d), F32)
    operands = [slot, ys, p, norm_w.reshape(1, d), w_gate, w_proj, post_w.reshape(1, d)]
    if not final:
        proj_spec, proj_shape = _proj_out(n, next_w_in.shape[1], split, lambda i, slot: i)
        in_specs.append(pl.BlockSpec(next_w_in.shape, const))
        out_specs, out_shape = [out_specs, proj_spec], [out_shape, proj_shape]
        operands.append(next_w_in)
    grid_spec = pltpu.PrefetchScalarGridSpec(
        num_scalar_prefetch=1,
        grid=(n // ROW_TILE,),
        in_specs=in_specs,
        out_specs=out_specs,
        scratch_shapes=[pltpu.VMEM((2, ROW_TILE, d), F32),
                        pltpu.SemaphoreType.DMA((2,))],
    )
    return pl.pallas_call(
        functools.partial(_ple_kernel, final=final),
        grid_spec=grid_spec,
        out_shape=out_shape,
        compiler_params=_params("arbitrary"),
        name="per_layer_embedding",
    )(*operands)


def kernel(x, p, attn_norm_w, ffn_norm_w, final_norm_w, even_w_in, even_w_out, ret_norm_w, sg_norm_w, sg_spatial_w, sg_spatial_b, odd_w_in, odd_w_out, moe_w_group, moe_b_group, moe_w_expert, moe_b_expert, moe_w_gate, moe_w_up, moe_w_down, ple_norm_w, ple_w_gate, ple_w_proj):
    batch, seq, d = x.shape
    depth = p.shape[0]
    n = batch * seq
    n_slots = n + MOE_BUCKETS * MOE_TILE

    def w_in(layer):
        odd = layer % 2 == 1
        return (odd_w_in if odd else even_w_in)[layer // 2].astype(BF16), odd

    h = x.reshape(n, d)
    sorted_rows = jnp.zeros((n_slots, d), F32)
    proj = _norm_matmul(h, attn_norm_w[0], *w_in(0))
    for i in range(depth):
        j = i // 2
        if i % 2 == 0:
            h = _even_mixer(h, proj, batch, seq, even_w_out[j].astype(BF16), ret_norm_w[j], sg_norm_w[j],
                            sg_spatial_w[j], sg_spatial_b[j])
        else:
            att = _stick_breaking(proj, batch, seq)
            h = _matmul_residual(att, odd_w_out[j].astype(BF16), h)

        w_route = jnp.concatenate([moe_w_group[i], moe_w_expert[i]], axis=1)
        w_route = jnp.pad(w_route, ((0, 0), (0, LANES - w_route.shape[1])))
        whi, wlo = _split_bf16(w_route)
        bias = jnp.concatenate([moe_b_group[i], moe_b_expert[i]])
        bias = jnp.pad(bias, (0, LANES - bias.shape[0])).reshape(1, LANES)
        bucket, rank, counts = _router(h, ffn_norm_w[i], whi, wlo, bias)
        start, tables = _tile_tables(counts, n)
        slot = _slots(bucket, rank, start).reshape(n)
        xs = _dispatch(h, slot, sorted_rows)
        sorted_rows = _moe(xs, ffn_norm_w[i], whi, bias, moe_w_gate[i].astype(BF16), moe_w_up[i].astype(BF16),
                           moe_w_down[i].astype(BF16), tables)
        ple = (sorted_rows, slot, p[i].reshape(n, -1), ple_norm_w[i], ple_w_gate[i].astype(BF16),
               ple_w_proj[i].astype(BF16))
        if i == depth - 1:
            h = _ple(*ple, final_norm_w)
        else:
            h, proj = _ple(*ple, attn_norm_w[i + 1], *w_in(i + 1))
    return h.reshape(batch, seq, d)
```

```python
import functools

import jax
import jax.numpy as jnp
from jax import lax
from jax.experimental import pallas as pl
from jax.experimental.pallas import tpu as pltpu

F32 = jnp.float32
BF16 = jnp.bfloat16
I32 = jnp.int32

LANES = 128
CHUNK = 128
RET_HEADS = 4
RET_DIM = 128
RET_WIDTH = RET_HEADS * RET_DIM
SG_GROUPS = 4
SG_DIM = 128
SG_WIDTH = SG_GROUPS * SG_DIM
SB_DIM = 64
MOE_GROUPS = 4
MOE_PER_GROUP = 8
MOE_EXPERTS = MOE_GROUPS * MOE_PER_GROUP
MOE_PAIRS = MOE_PER_GROUP * (MOE_PER_GROUP - 1) // 2
MOE_BUCKETS = MOE_GROUPS * MOE_PAIRS
ROPE_BASE = 10000.0
EPS = 1e-6

ROW_TILE = 512
MIX_TILE = 512
PROJ_CHUNK = 512
MOE_TILE = 256
ATT_Q = 256
EXP_UNDERFLOW = -104.0
DMA_UNROLL = 8
DMA_THREADS = 2
VMEM_LIMIT = 56 * 1024 * 1024


def _params(*sem):
    return pltpu.CompilerParams(dimension_semantics=sem, vmem_limit_bytes=VMEM_LIMIT)


def _rms(x, w):
    return x * lax.rsqrt(jnp.mean(x * x, axis=-1, keepdims=True) + EPS) * w


def _dot(a, b):
    return jnp.dot(a, b, preferred_element_type=F32)


def _dot_nt(a, b):
    return lax.dot_general(a, b, (((1,), (1,)), ((), ())), preferred_element_type=F32)


def _dot_tn(a, b):
    return lax.dot_general(a, b, (((0,), (0,)), ((), ())), preferred_element_type=F32)


def _aligned(index, multiple):
    return index if isinstance(index, int) else pl.multiple_of(index, multiple)


def _split_bf16(x):
    hi = x.astype(BF16)
    lo = (x - hi.astype(F32)).astype(BF16)
    return hi, lo


def _project(x, nw_ref, w_ref, o_ref, rows=slice(None)):
    xn = _rms(x, nw_ref[...]).astype(BF16)
    for c in range(0, w_ref.shape[1], PROJ_CHUNK):
        res = _dot(xn, w_ref[:, c:c + PROJ_CHUNK]).astype(o_ref.dtype)
        if len(o_ref.shape) == 3:
            for s in range(PROJ_CHUNK // LANES):
                o_ref[c // LANES + s, rows, :] = res[:, s * LANES:(s + 1) * LANES]
        else:
            o_ref[rows, c:c + PROJ_CHUNK] = res


def _proj_out(n, m, split, index):
    if split:
        return (pl.BlockSpec((m // LANES, ROW_TILE, LANES), lambda *a: (0, index(*a), 0)),
                jax.ShapeDtypeStruct((m // LANES, n, LANES), BF16))
    return (pl.BlockSpec((ROW_TILE, m), lambda *a: (index(*a), 0)), jax.ShapeDtypeStruct((n, m), BF16))


def _norm_matmul_kernel(h_ref, nw_ref, w_ref, o_ref):
    _project(h_ref[...], nw_ref, w_ref, o_ref)


def _norm_matmul(h, norm_w, w, split):
    n, d = h.shape
    out_spec, out_shape = _proj_out(n, w.shape[1], split, lambda i: i)
    return pl.pallas_call(
        _norm_matmul_kernel,
        grid=(n // ROW_TILE,),
        in_specs=[pl.BlockSpec((ROW_TILE, d), lambda i: (i, 0)),
                  pl.BlockSpec((1, d), lambda i: (0, 0)),
                  pl.BlockSpec(w.shape, lambda i: (0, 0))],
        out_specs=out_spec,
        out_shape=out_shape,
        compiler_params=_params("parallel"),
        name="norm_matmul",
    )(h, norm_w.reshape(1, d), w)


def _matmul_residual_kernel(a_ref, w_ref, h_ref, o_ref):
    a = jnp.concatenate([a_ref[c] for c in range(a_ref.shape[0])], axis=1)
    o_ref[...] = h_ref[...] + _dot(a, w_ref[...])


def _matmul_residual(a, w, h):
    groups, n, _ = a.shape
    d = w.shape[1]
    return pl.pallas_call(
        _matmul_residual_kernel,
        grid=(n // ROW_TILE,),
        in_specs=[pl.BlockSpec((groups, ROW_TILE, LANES), lambda i: (0, i, 0)),
                  pl.BlockSpec(w.shape, lambda i: (0, 0)),
                  pl.BlockSpec((ROW_TILE, d), lambda i: (i, 0))],
        out_specs=pl.BlockSpec((ROW_TILE, d), lambda i: (i, 0)),
        out_shape=jax.ShapeDtypeStruct((n, d), F32),
        compiler_params=_params("parallel"),
        name="matmul_residual",
    )(a, w, h)


def _even_mixer_kernel(h_ref, proj_ref, cos_ref, sin_ref, intra_ref, qdec_ref, kdec_ref, cdec_ref,
                       retw_ref, sgnw_ref, sgw_ref, sgb_ref, wout_ref, o_ref, state_ref, mixed_ref):
    @pl.when(pl.program_id(1) == 0)
    def _():
        state_ref[...] = jnp.zeros_like(state_ref)

    def chunk(c, carry):
        r0 = pl.multiple_of(c * CHUNK, CHUNK)
        rows = pl.ds(r0, CHUNK)
        cos = cos_ref[rows, :]
        sin = sin_ref[rows, :]

        def rope(t):
            return t * cos + pltpu.roll(t, RET_DIM // 2, 1) * sin

        heads = range(RET_HEADS)
        lanes = [slice(hd * RET_DIM, (hd + 1) * RET_DIM) for hd in heads]

        def field(idx, hd):
            return proj_ref[rows, idx * RET_WIDTH + hd * RET_DIM:idx * RET_WIDTH + (hd + 1) * RET_DIM]

        q = [rope(field(0, hd).astype(F32)) for hd in heads]
        k = [rope(field(1, hd).astype(F32)) for hd in heads]
        v = [field(2, hd) for hd in heads]
        scores = [_dot_nt(q[hd].astype(BF16), k[hd].astype(BF16)) * intra_ref[hd] for hd in heads]
        st = [state_ref[hd] for hd in heads]
        out = [_dot(scores[hd].astype(BF16), v[hd]) + _dot((q[hd] * qdec_ref[hd]).astype(BF16), st[hd].astype(BF16))
               for hd in heads]
        for hd in heads:
            state_ref[hd] = st[hd] * cdec_ref[hd] + _dot_tn((k[hd] * kdec_ref[hd]).astype(BF16), v[hd])
        for hd in heads:
            mu = jnp.mean(out[hd], axis=-1, keepdims=True)
            cen = out[hd] - mu
            var = jnp.mean(cen * cen, axis=-1, keepdims=True)
            ret = cen * lax.rsqrt(var + EPS) * retw_ref[:, lanes[hd]]
            mixed_ref[rows, lanes[hd]] = (jax.nn.silu(field(3, hd).astype(F32)) * ret).astype(BF16)

        base = 4 * RET_WIDTH
        gv = jax.nn.gelu(proj_ref[rows, base + SG_WIDTH:base + 2 * SG_WIDTH].astype(F32))
        vsn = _rms(gv, sgnw_ref[...]).astype(BF16)
        for gi in range(SG_GROUPS):
            lanes = slice(gi * SG_DIM, (gi + 1) * SG_DIM)
            u = proj_ref[rows, base + gi * SG_DIM:base + (gi + 1) * SG_DIM].astype(F32)
            mix = _dot(sgw_ref[gi], vsn[:, lanes]) + sgb_ref[gi]
            mixed_ref[rows, RET_WIDTH + gi * SG_DIM:RET_WIDTH + (gi + 1) * SG_DIM] = (
                jax.nn.gelu(u) * mix).astype(BF16)
        return carry

    lax.fori_loop(0, h_ref.shape[0] // CHUNK, chunk, 0)
    o_ref[...] = h_ref[...] + _dot(mixed_ref[...], wout_ref[...])


def _even_mixer(h, proj, batch, seq, w_out, ret_norm_w, sg_norm_w, sg_w, sg_b):
    n, d = h.shape
    steps = seq // MIX_TILE
    inv_freq = ROPE_BASE ** (-jnp.arange(0, RET_DIM, 2, dtype=F32) / RET_DIM)
    ang = jnp.arange(seq, dtype=F32)[:, None] * inv_freq[None, :]
    cos = jnp.concatenate([jnp.cos(ang), jnp.cos(ang)], axis=-1)
    sin = jnp.concatenate([-jnp.sin(ang), jnp.sin(ang)], axis=-1)
    log_gamma = jnp.log1p(-jnp.exp2(-5.0 - jnp.arange(RET_HEADS, dtype=F32)))
    pos = jnp.arange(CHUNK, dtype=F32)
    diff = pos[:, None] - pos[None, :]
    scale = RET_DIM ** -0.5
    intra = jnp.where(diff >= 0, jnp.exp(log_gamma[:, None, None] * jnp.maximum(diff, 0.0)), 0.0) * scale
    bshape = (RET_HEADS, CHUNK, RET_DIM)
    qdec = jnp.broadcast_to(jnp.exp(log_gamma[:, None] * (pos[None, :] + 1.0))[:, :, None], bshape)
    kdec = jnp.broadcast_to(jnp.exp(log_gamma[:, None] * (CHUNK - 1.0 - pos[None, :]))[:, :, None] * scale, bshape)
    cdec = jnp.broadcast_to(jnp.exp(log_gamma * CHUNK)[:, None, None], bshape)
    tril = jnp.tril(jnp.ones((CHUNK, CHUNK), dtype=bool))
    sgw = jnp.where(tril, sg_w, jnp.zeros_like(sg_w)).astype(BF16)
    sgb = jnp.broadcast_to(sg_b[:, :, None], (SG_GROUPS, CHUNK, SG_DIM))

    const3 = lambda b, s: (0, 0, 0)
    const2 = lambda b, s: (0, 0)
    row = lambda b, s: (b * steps + s, 0)
    return pl.pallas_call(
        _even_mixer_kernel,
        grid=(batch, steps),
        in_specs=[pl.BlockSpec((MIX_TILE, d), row),
                  pl.BlockSpec((MIX_TILE, proj.shape[1]), row),
                  pl.BlockSpec((MIX_TILE, RET_DIM), lambda b, s: (s, 0)),
                  pl.BlockSpec((MIX_TILE, RET_DIM), lambda b, s: (s, 0)),
                  pl.BlockSpec(bshape, const3),
                  pl.BlockSpec(bshape, const3),
                  pl.BlockSpec(bshape, const3),
                  pl.BlockSpec(bshape, const3),
                  pl.BlockSpec((1, RET_WIDTH), const2),
                  pl.BlockSpec((1, SG_WIDTH), const2),
                  pl.BlockSpec((SG_GROUPS, CHUNK, CHUNK), const3),
                  pl.BlockSpec((SG_GROUPS, CHUNK, SG_DIM), const3),
                  pl.BlockSpec(w_out.shape, const2)],
        out_specs=pl.BlockSpec((MIX_TILE, d), row),
        out_shape=jax.ShapeDtypeStruct((n, d), F32),
        scratch_shapes=[pltpu.VMEM((RET_HEADS, RET_DIM, RET_DIM), F32),
                        pltpu.VMEM((MIX_TILE, RET_WIDTH + SG_WIDTH), BF16)],
        compiler_params=_params("parallel", "arbitrary"),
        name="even_mixer",
    )(h, proj, cos, sin, intra, qdec, kdec, cdec, ret_norm_w.reshape(1, -1), sg_norm_w.reshape(1, -1),
      sgw, sgb, w_out)


def _stick_kernel(q_ref, k_ref, v_ref, cum_ref, o_ref, qs_ref):
    seq = q_ref.shape[0]
    n_sub = ATT_Q // CHUNK
    row = lax.broadcasted_iota(I32, (CHUNK, CHUNK), 0)
    lane = lax.broadcasted_iota(I32, (CHUNK, CHUNK), 1)
    tri = lane < row
    first_k = lane < SB_DIM
    first_q = lax.broadcasted_iota(I32, (ATT_Q, LANES), 1) < SB_DIM
    cum = cum_ref[...]

    def run(jobs, carry, acc):
        stacked = []
        for z, _, _, diagonal, _, _ in jobs:
            stay = jnp.minimum(-z, 0.0) - jnp.log(1.0 + jnp.exp(-jnp.abs(z)))
            if diagonal:
                stay = jnp.where(tri, stay, 0.0)
            stacked.append(stay.astype(BF16))
        sums = [_dot(lhs, cum) for lhs in stacked]
        weights = []
        for (z, s, head, diagonal, _, live), both in zip(jobs, sums):
            c = head * n_sub + s
            a = jnp.exp(z + carry[c] + both[:, :CHUNK])
            block_sum = both[:, CHUNK:]
            if diagonal:
                a = jnp.where(tri, a, 0.0)
            if live is not None:
                a = a * live
                block_sum = block_sum * live
            carry[c] = carry[c] + block_sum
            weights.append(a.astype(BF16))
        for (_, s, _, _, v_head, _), a in zip(jobs, weights):
            acc[s] = acc[s] + _dot(a, v_head)

    def load_kv(j):
        keys = pl.ds(_aligned(j * CHUNK, CHUNK), CHUNK)
        kb = k_ref[keys, :]
        vb = v_ref[keys, :]
        zero = jnp.zeros_like(vb)
        return kb, (jnp.where(first_k, vb, zero), jnp.where(first_k, zero, vb))

    def q_rows(s, head):
        r = head * ATT_Q + s * CHUNK
        return slice(r, r + CHUNK)

    def tile(t, first):
        r0 = _aligned(t * ATT_Q, ATT_Q)
        q = q_ref[pl.ds(r0, ATT_Q), :] * (SB_DIM ** -0.5)
        zero = jnp.zeros_like(q)
        qs_ref[0:ATT_Q, :] = jnp.where(first_q, q, zero)
        qs_ref[ATT_Q:, :] = jnp.where(first_q, zero, q)
        blank = jnp.zeros((CHUNK, CHUNK), F32)
        carry = [blank] * (2 * n_sub)
        acc = [blank] * n_sub

        jobs = []
        for jd in reversed(range(n_sub)):
            kb, v_heads = load_kv(t * n_sub + jd)
            for s in range(jd, n_sub):
                for head in range(2):
                    jobs.append((_dot_nt(qs_ref[q_rows(s, head), :], kb), s, head, s == jd, v_heads[head], None))
        if not first:
            for b in range(n_sub):
                kb, v_heads = load_kv(t * n_sub - 1 - b)
                z_all = _dot_nt(qs_ref[...], kb)
                for s in range(n_sub - b):
                    for head in range(2):
                        jobs.append((z_all[q_rows(s, head), :], s, head, False, v_heads[head], None))
        run(jobs, carry, acc)

        def top_carry(carry):
            top = carry[0]
            for other in carry[1:]:
                top = jnp.maximum(top, other)
            return jnp.max(top)

        def older(state):
            trip, _, carry, acc = state
            carry, acc = list(carry), list(acc)
            jobs = []
            for s in range(n_sub):
                j = (t - 1) * n_sub + s - 1 - trip
                live = (j >= 0).astype(F32)
                kb, v_heads = load_kv(jnp.maximum(j, 0))
                for head in range(2):
                    jobs.append((_dot_nt(qs_ref[q_rows(s, head), :], kb), s, head, False, v_heads[head], live))
            run(jobs, carry, acc)
            return trip + 1, top_carry(carry), tuple(carry), tuple(acc)

        def unfinished(state):
            return (state[0] < t * n_sub - 1) & (state[1] > EXP_UNDERFLOW)

        if not first:
            start = (jnp.int32(0), top_carry(carry), tuple(carry), tuple(acc))
            acc = lax.while_loop(unfinished, older, start)[3]
        for s in range(n_sub):
            o_ref[pl.ds(r0 + s * CHUNK, CHUNK), :] = acc[s].astype(o_ref.dtype)

    tile(0, True)

    def later_tile(t, c):
        tile(t, False)
        return c

    lax.fori_loop(1, seq // ATT_Q, later_tile, 0)


def _stick_breaking(qkv, batch, seq):
    pairs = qkv.shape[0] // 3
    n = qkv.shape[1]
    tri = (jnp.arange(CHUNK)[:, None] >= jnp.arange(CHUNK)[None, :])
    cum = jnp.concatenate([tri.astype(BF16), jnp.ones((CHUNK, CHUNK), BF16)], axis=1)
    blk = (None, seq, LANES)
    return pl.pallas_call(
        _stick_kernel,
        grid=(batch, pairs),
        in_specs=[pl.BlockSpec(blk, lambda b, p: (p, b, 0)),
                  pl.BlockSpec(blk, lambda b, p: (pairs + p, b, 0)),
                  pl.BlockSpec(blk, lambda b, p: (2 * pairs + p, b, 0)),
                  pl.BlockSpec((CHUNK, 2 * CHUNK), lambda b, p: (0, 0))],
        out_specs=pl.BlockSpec(blk, lambda b, p: (p, b, 0)),
        out_shape=jax.ShapeDtypeStruct((pairs, n, LANES), BF16),
        scratch_shapes=[pltpu.VMEM((2 * ATT_Q, LANES), BF16)],
        compiler_params=_params("parallel", "parallel"),
        name="stick_breaking",
    )(qkv, qkv, qkv, cum)


def _router_logits(xn, whi_ref, wlo_ref, b_ref):
    hi, lo = _split_bf16(xn)
    return _dot(hi, whi_ref[...]) + _dot(hi, wlo_ref[...]) + _dot(lo, whi_ref[...]) + b_ref[...]


def _router_kernel(h_ref, nw_ref, whi_ref, wlo_ref, b_ref, tri_ref, bucket_ref, rank_ref, counts_ref, run_ref):
    @pl.when(pl.program_id(0) == 0)
    def _():
        run_ref[...] = jnp.zeros_like(run_ref)

    logits = _router_logits(_rms(h_ref[...], nw_ref[...]), whi_ref, wlo_ref, b_ref)
    lane = lax.broadcasted_iota(I32, logits.shape, 1)
    neg = jnp.float32(-jnp.inf)

    def first_max(vals):
        m = jnp.max(vals, axis=-1, keepdims=True)
        return jnp.min(jnp.where(vals == m, lane, LANES), axis=-1, keepdims=True)

    group = first_max(jnp.where(lane < MOE_GROUPS, logits, neg))
    rel = lane - MOE_GROUPS
    in_group = (rel >= 0) & (rel < MOE_EXPERTS) & ((rel >> 3) == group)
    ev = jnp.where(in_group, logits, neg)
    i1 = first_max(ev)
    i2 = first_max(jnp.where(lane == i1, neg, ev))
    a = (i1 - MOE_GROUPS) & (MOE_PER_GROUP - 1)
    b = (i2 - MOE_GROUPS) & (MOE_PER_GROUP - 1)
    lo = jnp.minimum(a, b)
    hi = jnp.maximum(a, b)
    pair = (lo * (2 * MOE_PER_GROUP - 1 - lo)) // 2 + (hi - lo - 1)
    bucket = group * MOE_PAIRS + pair

    onehot = lane == bucket
    before = _dot(tri_ref[...], onehot.astype(BF16)) + run_ref[...]
    rank = jnp.sum(jnp.where(onehot, before, 0.0), axis=-1, keepdims=True)
    run_ref[...] += jnp.sum(onehot.astype(F32), axis=0, keepdims=True)
    bucket_ref[...] = bucket
    rank_ref[...] = rank.astype(I32)
    counts_ref[...] = run_ref[...]


def _router(h, norm_w, whi, wlo, bias):
    n, d = h.shape
    const = lambda i: (0, 0)
    tri = (jnp.arange(ROW_TILE)[:, None] > jnp.arange(ROW_TILE)[None, :]).astype(BF16)
    col = jax.ShapeDtypeStruct((n, 1), I32)
    return pl.pallas_call(
        _router_kernel,
        grid=(n // ROW_TILE,),
        in_specs=[pl.BlockSpec((ROW_TILE, d), lambda i: (i, 0)),
                  pl.BlockSpec((1, d), const),
                  pl.BlockSpec((d, LANES), const),
                  pl.BlockSpec((d, LANES), const),
                  pl.BlockSpec((1, LANES), const),
                  pl.BlockSpec((ROW_TILE, ROW_TILE), const)],
        out_specs=[pl.BlockSpec((ROW_TILE, 1), lambda i: (i, 0)),
                   pl.BlockSpec((ROW_TILE, 1), lambda i: (i, 0)),
                   pl.BlockSpec((1, LANES), const)],
        out_shape=[col, col, jax.ShapeDtypeStruct((1, LANES), F32)],
        scratch_shapes=[pltpu.VMEM((1, LANES), F32)],
        compiler_params=_params("arbitrary"),
        name="moe_router",
    )(h, norm_w.reshape(1, d), whi, wlo, bias, tri)


def _slot_kernel(bucket_ref, rank_ref, start_ref, slot_ref):
    lane = lax.broadcasted_iota(I32, (bucket_ref.shape[0], LANES), 1)
    start = jnp.sum(jnp.where(lane == bucket_ref[...], start_ref[...], 0), axis=-1, keepdims=True)
    slot_ref[...] = start + rank_ref[...]


def _slots(bucket, rank, start):
    n = bucket.shape[0]
    rows = min(n, 8 * ROW_TILE)
    return pl.pallas_call(
        _slot_kernel,
        grid=(n // rows,),
        in_specs=[pl.BlockSpec((rows, 1), lambda i: (i, 0)),
                  pl.BlockSpec((rows, 1), lambda i: (i, 0)),
                  pl.BlockSpec((1, LANES), lambda i: (0, 0))],
        out_specs=pl.BlockSpec((rows, 1), lambda i: (i, 0)),
        out_shape=jax.ShapeDtypeStruct((n, 1), I32),
        compiler_params=_params("parallel"),
        name="moe_slots",
    )(bucket, rank, start)


def _tile_tables(counts, n):
    t = MOE_TILE
    n_tiles = n // t + MOE_BUCKETS
    counts = counts.reshape(LANES).astype(I32)[:MOE_BUCKETS]
    tiles = (counts + t - 1) // t
    tile_end = jnp.cumsum(tiles)
    tile_start = tile_end - tiles
    used = tile_end[-1]
    tile_id = jnp.arange(n_tiles, dtype=I32)
    block = jnp.minimum(tile_id, used - 1)
    tile_bucket = jnp.sum((tile_end[None, :] <= block[:, None]).astype(I32), axis=1)
    onehot = (tile_bucket[:, None] == jnp.arange(MOE_BUCKETS, dtype=I32)[None, :]).astype(I32)
    bucket_count = jnp.sum(onehot * counts[None, :], axis=1)
    bucket_start = jnp.sum(onehot * tile_start[None, :], axis=1)
    rows = jnp.clip(bucket_count - (block - bucket_start) * t, 0, t)
    rows = jnp.where(tile_id < used, rows, 0)
    group = tile_bucket // MOE_PAIRS
    pair = tile_bucket % MOE_PAIRS
    lo_tab, hi_tab = [], []
    for lo in range(MOE_PER_GROUP):
        for hi in range(lo + 1, MOE_PER_GROUP):
            lo_tab.append(lo)
            hi_tab.append(hi)
    pair_hot = (pair[:, None] == jnp.arange(MOE_PAIRS, dtype=I32)[None, :]).astype(I32)
    e_lo = group * MOE_PER_GROUP + jnp.sum(pair_hot * jnp.asarray(lo_tab, I32)[None, :], axis=1)
    e_hi = group * MOE_PER_GROUP + jnp.sum(pair_hot * jnp.asarray(hi_tab, I32)[None, :], axis=1)
    start = jnp.pad(tile_start * t, (0, LANES - MOE_BUCKETS)).reshape(1, LANES)
    return start, (block, rows.astype(I32), group, e_lo, e_hi)


def _dispatch_kernel(slot_ref, h_ref, init_hbm, out_hbm, sem):
    del init_hbm
    rows = h_ref.shape[0]
    base = pl.program_id(0) * rows

    for r in range(rows):
        pltpu.make_async_copy(h_ref.at[r], out_hbm.at[slot_ref[base + r]], sem).start(priority=r % DMA_THREADS)
    pltpu.make_async_copy(h_ref, out_hbm.at[pl.ds(0, rows)], sem).wait()


def _dispatch(h, slot, init):
    n, d = h.shape
    grid_spec = pltpu.PrefetchScalarGridSpec(
        num_scalar_prefetch=1,
        grid=(n // ROW_TILE,),
        in_specs=[pl.BlockSpec((ROW_TILE, d), lambda i, slot: (i, 0)),
                  pl.BlockSpec(memory_space=pl.ANY)],
        out_specs=pl.BlockSpec(memory_space=pl.ANY),
        scratch_shapes=[pltpu.SemaphoreType.DMA(())],
    )
    return pl.pallas_call(
        _dispatch_kernel,
        grid_spec=grid_spec,
        out_shape=jax.ShapeDtypeStruct(init.shape, F32),
        input_output_aliases={2: 0},
        compiler_params=_params("arbitrary"),
        name="moe_dispatch",
    )(slot, h, init)


def _moe_kernel(block_ref, rows_ref, group_ref, elo_ref, ehi_ref,
                x_ref, nw_ref, wr_ref, b_ref, wg1_ref, wu1_ref, wd1_ref, wg2_ref, wu2_ref, wd2_ref,
                o_ref):
    i = pl.program_id(0)

    @pl.when(rows_ref[i] == 0)
    def _():
        o_ref[...] = jnp.zeros_like(o_ref)

    @pl.when(rows_ref[i] > 0)
    def _():
        x = x_ref[...]
        xb = _rms(x, nw_ref[...]).astype(BF16)
        logits = _dot(xb, wr_ref[...]) + b_ref[...]
        lane = lax.broadcasted_iota(I32, logits.shape, 1)

        def pick(idx):
            return jnp.sum(jnp.where(lane == idx, logits, 0.0), axis=-1, keepdims=True)

        lg = pick(group_ref[i])
        gexp = jnp.exp(jnp.where(lane < MOE_GROUPS, logits - lg, -jnp.inf))
        group_w = 1.0 / jnp.sum(gexp, axis=-1, keepdims=True)
        l1 = pick(MOE_GROUPS + elo_ref[i])
        l2 = pick(MOE_GROUPS + ehi_ref[i])
        m = jnp.maximum(l1, l2)
        e1 = jnp.exp(l1 - m)
        e2 = jnp.exp(l2 - m)
        w1 = e1 / (e1 + e2) * group_w
        w2 = e2 / (e1 + e2) * group_w

        hid1 = jax.nn.silu(_dot(xb, wg1_ref[...])) * _dot(xb, wu1_ref[...]) * w1
        hid2 = jax.nn.silu(_dot(xb, wg2_ref[...])) * _dot(xb, wu2_ref[...]) * w2
        o_ref[...] = x + _dot(hid1.astype(BF16), wd1_ref[...]) + _dot(hid2.astype(BF16), wd2_ref[...])


def _moe(xs, norm_w, w_route, bias, w_gate, w_up, w_down, tables):
    n_slots, d = xs.shape
    hidden = w_gate.shape[2]
    const = lambda i, *_: (0, 0)
    tile = lambda i, block, rows, group, elo, ehi: (block[i], 0)
    lo_w = lambda i, block, rows, group, elo, ehi: (elo[i], 0, 0)
    hi_w = lambda i, block, rows, group, elo, ehi: (ehi[i], 0, 0)
    grid_spec = pltpu.PrefetchScalarGridSpec(
        num_scalar_prefetch=5,
        grid=(n_slots // MOE_TILE,),
        in_specs=[pl.BlockSpec((MOE_TILE, d), tile),
                  pl.BlockSpec((1, d), const),
                  pl.BlockSpec((d, LANES), const),
                  pl.BlockSpec((1, LANES), const),
                  pl.BlockSpec((None, d, hidden), lo_w),
                  pl.BlockSpec((None, d, hidden), lo_w),
                  pl.BlockSpec((None, hidden, d), lo_w),
                  pl.BlockSpec((None, d, hidden), hi_w),
                  pl.BlockSpec((None, d, hidden), hi_w),
                  pl.BlockSpec((None, hidden, d), hi_w)],
        out_specs=pl.BlockSpec((MOE_TILE, d), lambda i, *_: (i, 0)),
    )
    return pl.pallas_call(
        _moe_kernel,
        grid_spec=grid_spec,
        out_shape=jax.ShapeDtypeStruct((n_slots, d), F32),
        compiler_params=_params("arbitrary"),
        name="sparse_moe",
    )(*tables, xs, norm_w.reshape(1, d), w_route, bias, w_gate, w_up, w_down, w_gate, w_up, w_down)


def _ple_kernel(slot_ref, ys_hbm, p_ref, nw_ref, wg_ref, wp_ref, post_ref, *rest, final):
    if final:
        o_ref, buf_a, buf_b, sem = rest
    else:
        win_ref, o_ref, proj_ref, buf_a, buf_b, sem = rest
    i = pl.program_id(0)
    last = pl.num_programs(0) - 1
    half = o_ref.shape[0] // 2
    bufs = (buf_a, buf_b)

    def wait(b):
        pltpu.make_async_copy(ys_hbm.at[pl.ds(0, half)], bufs[b], sem.at[b]).wait()

    @pl.when(i == 0)
    def _():
        def issue(r, c):
            pltpu.make_async_copy(ys_hbm.at[slot_ref[r]], buf_a.at[r], sem.at[0]).start()
            return c

        lax.fori_loop(0, half, issue, 0, unroll=DMA_UNROLL)

    def compute(b, fetch_base):
        groups = 4
        per = half // groups

        def fetch(g):
            for r in range(g * per, (g + 1) * per):
                pltpu.make_async_copy(ys_hbm.at[slot_ref[fetch_base + r]], bufs[1 - b].at[r],
                                      sem.at[1 - b]).start(priority=r % DMA_THREADS)

        rows = slice(b * half, (b + 1) * half)
        d = wg_ref.shape[1]
        x = bufs[b][...]
        xn = _rms(x, nw_ref[...]).astype(BF16)
        fetch(0)
        gate_lo = _dot(xn, wg_ref[:, :d // 2])
        fetch(1)
        gate_hi = _dot(xn, wg_ref[:, d // 2:])
        fetch(2)
        emb = _dot(p_ref[rows, :].astype(BF16), wp_ref[...])
        fetch(3)
        out = x + jax.nn.sigmoid(jnp.concatenate([gate_lo, gate_hi], axis=1)) * emb
        if final:
            o_ref[rows, :] = _rms(out, post_ref[...])
        else:
            o_ref[rows, :] = out
            _project(out, post_ref, win_ref, proj_ref, rows)

    step_rows = 2 * half
    wait(0)
    compute(0, i * step_rows + half)
    wait(1)
    compute(1, jnp.minimum(i + 1, last) * step_rows)

    @pl.when(i == last)
    def _():
        wait(0)


def _ple(ys, slot, p, norm_w, w_gate, w_proj, post_w, next_w_in=None, split=False):
    n = p.shape[0]
    d = ys.shape[1]
    final = next_w_in is None
    const = lambda i, slot: (0, 0)
    in_specs = [pl.BlockSpec(memory_space=pl.ANY),
                pl.BlockSpec((ROW_TILE, p.shape[1]), lambda i, slot: (i, 0)),
                pl.BlockSpec((1, d), const),
                pl.BlockSpec(w_gate.shape, const),
                pl.BlockSpec(w_proj.shape, const),
                pl.BlockSpec((1, d), const)]
    out_specs = pl.BlockSpec((ROW_TILE, d), lambda i, slot: (i, 0))
    out_shape = jax.ShapeDtypeStruct((n, d), F32)
    operands = [slot, ys, p, norm_w.reshape(1, d), w_gate, w_proj, post_w.reshape(1, d)]
    if not final:
        proj_spec, proj_shape = _proj_out(n, next_w_in.shape[1], split, lambda i, slot: i)
        in_specs.append(pl.BlockSpec(next_w_in.shape, const))
        out_specs, out_shape = [out_specs, proj_spec], [out_shape, proj_shape]
        operands.append(next_w_in)
    grid_spec = pltpu.PrefetchScalarGridSpec(
        num_scalar_prefetch=1,
        grid=(n // ROW_TILE,),
        in_specs=in_specs,
        out_specs=out_specs,
        scratch_shapes=[pltpu.VMEM((ROW_TILE // 2, d), F32),
                        pltpu.VMEM((ROW_TILE // 2, d), F32),
                        pltpu.SemaphoreType.DMA((2,))],
    )
    return pl.pallas_call(
        functools.partial(_ple_kernel, final=final),
        grid_spec=grid_spec,
        out_shape=out_shape,
        compiler_params=_params("arbitrary"),
        name="per_layer_embedding",
    )(*operands)


def kernel(x, p, attn_norm_w, ffn_norm_w, final_norm_w, even_w_in, even_w_out, ret_norm_w, sg_norm_w, sg_spatial_w, sg_spatial_b, odd_w_in, odd_w_out, moe_w_group, moe_b_group, moe_w_expert, moe_b_expert, moe_w_gate, moe_w_up, moe_w_down, ple_norm_w, ple_w_gate, ple_w_proj):
    batch, seq, d = x.shape
    depth = p.shape[0]
    n = batch * seq
    n_slots = n + MOE_BUCKETS * MOE_TILE

    def w_in(layer):
        odd = layer % 2 == 1
        return (odd_w_in if odd else even_w_in)[layer // 2].astype(BF16), odd

    h = x.reshape(n, d)
    sorted_rows = jnp.zeros((n_slots, d), F32)
    proj = _norm_matmul(h, attn_norm_w[0], *w_in(0))
    for i in range(depth):
        j = i // 2
        if i % 2 == 0:
            h = _even_mixer(h, proj, batch, seq, even_w_out[j].astype(BF16), ret_norm_w[j], sg_norm_w[j],
                            sg_spatial_w[j], sg_spatial_b[j])
        else:
            att = _stick_breaking(proj, batch, seq)
            h = _matmul_residual(att, odd_w_out[j].astype(BF16), h)

        w_route = jnp.concatenate([moe_w_group[i], moe_w_expert[i]], axis=1)
        w_route = jnp.pad(w_route, ((0, 0), (0, LANES - w_route.shape[1])))
        whi, wlo = _split_bf16(w_route)
        bias = jnp.concatenate([moe_b_group[i], moe_b_expert[i]])
        bias = jnp.pad(bias, (0, LANES - bias.shape[0])).reshape(1, LANES)
        bucket, rank, counts = _router(h, ffn_norm_w[i], whi, wlo, bias)
        start, tables = _tile_tables(counts, n)
        slot = _slots(bucket, rank, start).reshape(n)
        xs = _dispatch(h, slot, sorted_rows)
        sorted_rows = _moe(xs, ffn_norm_w[i], whi, bias, moe_w_gate[i].astype(BF16), moe_w_up[i].astype(BF16),
                           moe_w_down[i].astype(BF16), tables)
        ple = (sorted_rows, slot, p[i].reshape(n, -1), ple_norm_w[i], ple_w_gate[i].astype(BF16),
               ple_w_proj[i].astype(BF16))
        if i == depth - 1:
            h = _ple(*ple, final_norm_w)
        else:
            h, proj = _ple(*ple, attn_norm_w[i + 1], *w_in(i + 1))
    return h.reshape(batch, seq, d)
```

```python
import functools

import jax
import jax.numpy as jnp
from jax import lax
from jax.experimental import pallas as pl
from jax.experimental.pallas import tpu as pltpu

F32 = jnp.float32
BF16 = jnp.bfloat16
I32 = jnp.int32

LANES = 128
CHUNK = 128
RET_HEADS = 4
RET_DIM = 128
RET_WIDTH = RET_HEADS * RET_DIM
SG_GROUPS = 4
SG_DIM = 128
SG_WIDTH = SG_GROUPS * SG_DIM
SB_DIM = 64
MOE_GROUPS = 4
MOE_PER_GROUP = 8
MOE_EXPERTS = MOE_GROUPS * MOE_PER_GROUP
MOE_PAIRS = MOE_PER_GROUP * (MOE_PER_GROUP - 1) // 2
MOE_BUCKETS = MOE_GROUPS * MOE_PAIRS
ROPE_BASE = 10000.0
EPS = 1e-6

ROW_TILE = 512
MIX_TILE = 512
PROJ_CHUNK = 512
PLE_TILE = 1024
PLE_PARTS = 4
PLE_AHEAD = 2
MOE_TILE = 256
ATT_Q = 256
EXP_UNDERFLOW = -104.0
DMA_UNROLL = 8
DMA_THREADS = 2
VMEM_LIMIT = 56 * 1024 * 1024


def _params(*sem):
    return pltpu.CompilerParams(dimension_semantics=sem, vmem_limit_bytes=VMEM_LIMIT)


def _rms(x, w):
    return x * lax.rsqrt(jnp.mean(x * x, axis=-1, keepdims=True) + EPS) * w


def _dot(a, b):
    return jnp.dot(a, b, preferred_element_type=F32)


def _dot_nt(a, b):
    return lax.dot_general(a, b, (((1,), (1,)), ((), ())), preferred_element_type=F32)


def _dot_tn(a, b):
    return lax.dot_general(a, b, (((0,), (0,)), ((), ())), preferred_element_type=F32)


def _aligned(index, multiple):
    return index if isinstance(index, int) else pl.multiple_of(index, multiple)


def _split_bf16(x):
    hi = x.astype(BF16)
    lo = (x - hi.astype(F32)).astype(BF16)
    return hi, lo


def _project(x, nw_ref, w_ref, o_ref, rows=slice(None)):
    xn = _rms(x, nw_ref[...]).astype(BF16)
    for c in range(0, w_ref.shape[1], PROJ_CHUNK):
        res = _dot(xn, w_ref[:, c:c + PROJ_CHUNK]).astype(o_ref.dtype)
        if len(o_ref.shape) == 3:
            for s in range(PROJ_CHUNK // LANES):
                o_ref[c // LANES + s, rows, :] = res[:, s * LANES:(s + 1) * LANES]
        else:
            o_ref[rows, c:c + PROJ_CHUNK] = res


def _proj_out(n, m, split, index, rows=ROW_TILE):
    if split:
        return (pl.BlockSpec((m // LANES, rows, LANES), lambda *a: (0, index(*a), 0)),
                jax.ShapeDtypeStruct((m // LANES, n, LANES), BF16))
    return (pl.BlockSpec((rows, m), lambda *a: (index(*a), 0)), jax.ShapeDtypeStruct((n, m), BF16))


def _norm_matmul_kernel(h_ref, nw_ref, w_ref, o_ref):
    _project(h_ref[...], nw_ref, w_ref, o_ref)


def _norm_matmul(h, norm_w, w, split):
    n, d = h.shape
    out_spec, out_shape = _proj_out(n, w.shape[1], split, lambda i: i)
    return pl.pallas_call(
        _norm_matmul_kernel,
        grid=(n // ROW_TILE,),
        in_specs=[pl.BlockSpec((ROW_TILE, d), lambda i: (i, 0)),
                  pl.BlockSpec((1, d), lambda i: (0, 0)),
                  pl.BlockSpec(w.shape, lambda i: (0, 0))],
        out_specs=out_spec,
        out_shape=out_shape,
        compiler_params=_params("parallel"),
        name="norm_matmul",
    )(h, norm_w.reshape(1, d), w)


def _matmul_residual_kernel(a_ref, w_ref, h_ref, o_ref):
    a = jnp.concatenate([a_ref[c] for c in range(a_ref.shape[0])], axis=1)
    o_ref[...] = h_ref[...] + _dot(a, w_ref[...])


def _matmul_residual(a, w, h):
    groups, n, _ = a.shape
    d = w.shape[1]
    return pl.pallas_call(
        _matmul_residual_kernel,
        grid=(n // ROW_TILE,),
        in_specs=[pl.BlockSpec((groups, ROW_TILE, LANES), lambda i: (0, i, 0)),
                  pl.BlockSpec(w.shape, lambda i: (0, 0)),
                  pl.BlockSpec((ROW_TILE, d), lambda i: (i, 0))],
        out_specs=pl.BlockSpec((ROW_TILE, d), lambda i: (i, 0)),
        out_shape=jax.ShapeDtypeStruct((n, d), F32),
        compiler_params=_params("parallel"),
        name="matmul_residual",
    )(a, w, h)


def _even_mixer_kernel(h_ref, proj_ref, cos_ref, sin_ref, intra_ref, qdec_ref, kdec_ref, cdec_ref,
                       retw_ref, sgnw_ref, sgw_ref, sgb_ref, wout_ref, o_ref, state_ref, mixed_ref):
    @pl.when(pl.program_id(1) == 0)
    def _():
        state_ref[...] = jnp.zeros_like(state_ref)

    def chunk(c, carry):
        r0 = pl.multiple_of(c * CHUNK, CHUNK)
        rows = pl.ds(r0, CHUNK)
        cos = cos_ref[rows, :]
        sin = sin_ref[rows, :]

        def rope(t):
            return t * cos + pltpu.roll(t, RET_DIM // 2, 1) * sin

        heads = range(RET_HEADS)
        lanes = [slice(hd * RET_DIM, (hd + 1) * RET_DIM) for hd in heads]

        def field(idx, hd):
            return proj_ref[rows, idx * RET_WIDTH + hd * RET_DIM:idx * RET_WIDTH + (hd + 1) * RET_DIM]

        q = [rope(field(0, hd).astype(F32)) for hd in heads]
        k = [rope(field(1, hd).astype(F32)) for hd in heads]
        v = [field(2, hd) for hd in heads]
        scores = [_dot_nt(q[hd].astype(BF16), k[hd].astype(BF16)) * intra_ref[hd] for hd in heads]
        st = [state_ref[hd] for hd in heads]
        out = [_dot(scores[hd].astype(BF16), v[hd]) + _dot((q[hd] * qdec_ref[hd]).astype(BF16), st[hd].astype(BF16))
               for hd in heads]
        for hd in heads:
            state_ref[hd] = st[hd] * cdec_ref[hd] + _dot_tn((k[hd] * kdec_ref[hd]).astype(BF16), v[hd])
        for hd in heads:
            mu = jnp.mean(out[hd], axis=-1, keepdims=True)
            cen = out[hd] - mu
            var = jnp.mean(cen * cen, axis=-1, keepdims=True)
            ret = cen * lax.rsqrt(var + EPS) * retw_ref[:, lanes[hd]]
            mixed_ref[rows, lanes[hd]] = (jax.nn.silu(field(3, hd).astype(F32)) * ret).astype(BF16)

        base = 4 * RET_WIDTH
        gv = jax.nn.gelu(proj_ref[rows, base + SG_WIDTH:base + 2 * SG_WIDTH].astype(F32))
        vsn = _rms(gv, sgnw_ref[...]).astype(BF16)
        for gi in range(SG_GROUPS):
            lanes = slice(gi * SG_DIM, (gi + 1) * SG_DIM)
            u = proj_ref[rows, base + gi * SG_DIM:base + (gi + 1) * SG_DIM].astype(F32)
            mix = _dot(sgw_ref[gi], vsn[:, lanes]) + sgb_ref[gi]
            mixed_ref[rows, RET_WIDTH + gi * SG_DIM:RET_WIDTH + (gi + 1) * SG_DIM] = (
                jax.nn.gelu(u) * mix).astype(BF16)
        return carry

    lax.fori_loop(0, h_ref.shape[0] // CHUNK, chunk, 0)
    o_ref[...] = h_ref[...] + _dot(mixed_ref[...], wout_ref[...])


def _even_mixer(h, proj, batch, seq, w_out, ret_norm_w, sg_norm_w, sg_w, sg_b):
    n, d = h.shape
    steps = seq // MIX_TILE
    inv_freq = ROPE_BASE ** (-jnp.arange(0, RET_DIM, 2, dtype=F32) / RET_DIM)
    ang = jnp.arange(seq, dtype=F32)[:, None] * inv_freq[None, :]
    cos = jnp.concatenate([jnp.cos(ang), jnp.cos(ang)], axis=-1)
    sin = jnp.concatenate([-jnp.sin(ang), jnp.sin(ang)], axis=-1)
    log_gamma = jnp.log1p(-jnp.exp2(-5.0 - jnp.arange(RET_HEADS, dtype=F32)))
    pos = jnp.arange(CHUNK, dtype=F32)
    diff = pos[:, None] - pos[None, :]
    scale = RET_DIM ** -0.5
    intra = jnp.where(diff >= 0, jnp.exp(log_gamma[:, None, None] * jnp.maximum(diff, 0.0)), 0.0) * scale
    bshape = (RET_HEADS, CHUNK, RET_DIM)
    qdec = jnp.broadcast_to(jnp.exp(log_gamma[:, None] * (pos[None, :] + 1.0))[:, :, None], bshape)
    kdec = jnp.broadcast_to(jnp.exp(log_gamma[:, None] * (CHUNK - 1.0 - pos[None, :]))[:, :, None] * scale, bshape)
    cdec = jnp.broadcast_to(jnp.exp(log_gamma * CHUNK)[:, None, None], bshape)
    tril = jnp.tril(jnp.ones((CHUNK, CHUNK), dtype=bool))
    sgw = jnp.where(tril, sg_w, jnp.zeros_like(sg_w)).astype(BF16)
    sgb = jnp.broadcast_to(sg_b[:, :, None], (SG_GROUPS, CHUNK, SG_DIM))

    const3 = lambda b, s: (0, 0, 0)
    const2 = lambda b, s: (0, 0)
    row = lambda b, s: (b * steps + s, 0)
    return pl.pallas_call(
        _even_mixer_kernel,
        grid=(batch, steps),
        in_specs=[pl.BlockSpec((MIX_TILE, d), row),
                  pl.BlockSpec((MIX_TILE, proj.shape[1]), row),
                  pl.BlockSpec((MIX_TILE, RET_DIM), lambda b, s: (s, 0)),
                  pl.BlockSpec((MIX_TILE, RET_DIM), lambda b, s: (s, 0)),
                  pl.BlockSpec(bshape, const3),
                  pl.BlockSpec(bshape, const3),
                  pl.BlockSpec(bshape, const3),
                  pl.BlockSpec(bshape, const3),
                  pl.BlockSpec((1, RET_WIDTH), const2),
                  pl.BlockSpec((1, SG_WIDTH), const2),
                  pl.BlockSpec((SG_GROUPS, CHUNK, CHUNK), const3),
                  pl.BlockSpec((SG_GROUPS, CHUNK, SG_DIM), const3),
                  pl.BlockSpec(w_out.shape, const2)],
        out_specs=pl.BlockSpec((MIX_TILE, d), row),
        out_shape=jax.ShapeDtypeStruct((n, d), F32),
        scratch_shapes=[pltpu.VMEM((RET_HEADS, RET_DIM, RET_DIM), F32),
                        pltpu.VMEM((MIX_TILE, RET_WIDTH + SG_WIDTH), BF16)],
        compiler_params=_params("parallel", "arbitrary"),
        name="even_mixer",
    )(h, proj, cos, sin, intra, qdec, kdec, cdec, ret_norm_w.reshape(1, -1), sg_norm_w.reshape(1, -1),
      sgw, sgb, w_out)


def _stick_kernel(q_ref, k_ref, v_ref, cum_ref, o_ref, qs_ref):
    seq = q_ref.shape[0]
    n_sub = ATT_Q // CHUNK
    row = lax.broadcasted_iota(I32, (CHUNK, CHUNK), 0)
    lane = lax.broadcasted_iota(I32, (CHUNK, CHUNK), 1)
    tri = lane < row
    first_k = lane < SB_DIM
    first_q = lax.broadcasted_iota(I32, (ATT_Q, LANES), 1) < SB_DIM
    cum = cum_ref[...]

    def run(jobs, carry, acc):
        stacked = []
        for z, _, _, diagonal, _, _ in jobs:
            stay = jnp.minimum(-z, 0.0) - jnp.log(1.0 + jnp.exp(-jnp.abs(z)))
            if diagonal:
                stay = jnp.where(tri, stay, 0.0)
            stacked.append(stay.astype(BF16))
        sums = [_dot(lhs, cum) for lhs in stacked]
        weights = []
        for (z, s, head, diagonal, _, live), both in zip(jobs, sums):
            c = head * n_sub + s
            a = jnp.exp(z + carry[c] + both[:, :CHUNK])
            block_sum = both[:, CHUNK:]
            if diagonal:
                a = jnp.where(tri, a, 0.0)
            if live is not None:
                a = a * live
                block_sum = block_sum * live
            carry[c] = carry[c] + block_sum
            weights.append(a.astype(BF16))
        for (_, s, _, _, v_head, _), a in zip(jobs, weights):
            acc[s] = acc[s] + _dot(a, v_head)

    def load_kv(j):
        keys = pl.ds(_aligned(j * CHUNK, CHUNK), CHUNK)
        kb = k_ref[keys, :]
        vb = v_ref[keys, :]
        zero = jnp.zeros_like(vb)
        return kb, (jnp.where(first_k, vb, zero), jnp.where(first_k, zero, vb))

    def q_rows(s, head):
        r = head * ATT_Q + s * CHUNK
        return slice(r, r + CHUNK)

    def tile(t, first):
        r0 = _aligned(t * ATT_Q, ATT_Q)
        q = q_ref[pl.ds(r0, ATT_Q), :] * (SB_DIM ** -0.5)
        zero = jnp.zeros_like(q)
        qs_ref[0:ATT_Q, :] = jnp.where(first_q, q, zero)
        qs_ref[ATT_Q:, :] = jnp.where(first_q, zero, q)
        blank = jnp.zeros((CHUNK, CHUNK), F32)
        carry = [blank] * (2 * n_sub)
        acc = [blank] * n_sub

        jobs = []
        for jd in reversed(range(n_sub)):
            kb, v_heads = load_kv(t * n_sub + jd)
            for s in range(jd, n_sub):
                for head in range(2):
                    jobs.append((_dot_nt(qs_ref[q_rows(s, head), :], kb), s, head, s == jd, v_heads[head], None))
        if not first:
            for b in range(n_sub):
                kb, v_heads = load_kv(t * n_sub - 1 - b)
                z_all = _dot_nt(qs_ref[...], kb)
                for s in range(n_sub - b):
                    for head in range(2):
                        jobs.append((z_all[q_rows(s, head), :], s, head, False, v_heads[head], None))
        run(jobs, carry, acc)

        def top_carry(carry):
            top = carry[0]
            for other in carry[1:]:
                top = jnp.maximum(top, other)
            return jnp.max(top)

        def older(state):
            trip, _, carry, acc = state
            carry, acc = list(carry), list(acc)
            jobs = []
            for s in range(n_sub):
                j = (t - 1) * n_sub + s - 1 - trip
                live = (j >= 0).astype(F32)
                kb, v_heads = load_kv(jnp.maximum(j, 0))
                for head in range(2):
                    jobs.append((_dot_nt(qs_ref[q_rows(s, head), :], kb), s, head, False, v_heads[head], live))
            run(jobs, carry, acc)
            return trip + 1, top_carry(carry), tuple(carry), tuple(acc)

        def unfinished(state):
            return (state[0] < t * n_sub - 1) & (state[1] > EXP_UNDERFLOW)

        if not first:
            start = (jnp.int32(0), top_carry(carry), tuple(carry), tuple(acc))
            acc = lax.while_loop(unfinished, older, start)[3]
        for s in range(n_sub):
            o_ref[pl.ds(r0 + s * CHUNK, CHUNK), :] = acc[s].astype(o_ref.dtype)

    tile(0, True)

    def later_tile(t, c):
        tile(t, False)
        return c

    lax.fori_loop(1, seq // ATT_Q, later_tile, 0)


def _stick_breaking(qkv, batch, seq):
    pairs = qkv.shape[0] // 3
    n = qkv.shape[1]
    tri = (jnp.arange(CHUNK)[:, None] >= jnp.arange(CHUNK)[None, :])
    cum = jnp.concatenate([tri.astype(BF16), jnp.ones((CHUNK, CHUNK), BF16)], axis=1)
    blk = (None, seq, LANES)
    return pl.pallas_call(
        _stick_kernel,
        grid=(batch, pairs),
        in_specs=[pl.BlockSpec(blk, lambda b, p: (p, b, 0)),
                  pl.BlockSpec(blk, lambda b, p: (pairs + p, b, 0)),
                  pl.BlockSpec(blk, lambda b, p: (2 * pairs + p, b, 0)),
                  pl.BlockSpec((CHUNK, 2 * CHUNK), lambda b, p: (0, 0))],
        out_specs=pl.BlockSpec(blk, lambda b, p: (p, b, 0)),
        out_shape=jax.ShapeDtypeStruct((pairs, n, LANES), BF16),
        scratch_shapes=[pltpu.VMEM((2 * ATT_Q, LANES), BF16)],
        compiler_params=_params("parallel", "parallel"),
        name="stick_breaking",
    )(qkv, qkv, qkv, cum)


def _router_logits(xn, whi_ref, wlo_ref, b_ref):
    hi, lo = _split_bf16(xn)
    return _dot(hi, whi_ref[...]) + _dot(hi, wlo_ref[...]) + _dot(lo, whi_ref[...]) + b_ref[...]


def _router_kernel(h_ref, nw_ref, whi_ref, wlo_ref, b_ref, tri_ref, bucket_ref, rank_ref, counts_ref, run_ref):
    @pl.when(pl.program_id(0) == 0)
    def _():
        run_ref[...] = jnp.zeros_like(run_ref)

    logits = _router_logits(_rms(h_ref[...], nw_ref[...]), whi_ref, wlo_ref, b_ref)
    lane = lax.broadcasted_iota(I32, logits.shape, 1)
    neg = jnp.float32(-jnp.inf)

    def first_max(vals):
        m = jnp.max(vals, axis=-1, keepdims=True)
        return jnp.min(jnp.where(vals == m, lane, LANES), axis=-1, keepdims=True)

    group = first_max(jnp.where(lane < MOE_GROUPS, logits, neg))
    rel = lane - MOE_GROUPS
    in_group = (rel >= 0) & (rel < MOE_EXPERTS) & ((rel >> 3) == group)
    ev = jnp.where(in_group, logits, neg)
    i1 = first_max(ev)
    i2 = first_max(jnp.where(lane == i1, neg, ev))
    a = (i1 - MOE_GROUPS) & (MOE_PER_GROUP - 1)
    b = (i2 - MOE_GROUPS) & (MOE_PER_GROUP - 1)
    lo = jnp.minimum(a, b)
    hi = jnp.maximum(a, b)
    pair = (lo * (2 * MOE_PER_GROUP - 1 - lo)) // 2 + (hi - lo - 1)
    bucket = group * MOE_PAIRS + pair

    onehot = lane == bucket
    before = _dot(tri_ref[...], onehot.astype(BF16)) + run_ref[...]
    rank = jnp.sum(jnp.where(onehot, before, 0.0), axis=-1, keepdims=True)
    run_ref[...] += jnp.sum(onehot.astype(F32), axis=0, keepdims=True)
    bucket_ref[...] = bucket
    rank_ref[...] = rank.astype(I32)
    counts_ref[...] = run_ref[...]


def _router(h, norm_w, whi, wlo, bias):
    n, d = h.shape
    const = lambda i: (0, 0)
    tri = (jnp.arange(ROW_TILE)[:, None] > jnp.arange(ROW_TILE)[None, :]).astype(BF16)
    col = jax.ShapeDtypeStruct((n, 1), I32)
    return pl.pallas_call(
        _router_kernel,
        grid=(n // ROW_TILE,),
        in_specs=[pl.BlockSpec((ROW_TILE, d), lambda i: (i, 0)),
                  pl.BlockSpec((1, d), const),
                  pl.BlockSpec((d, LANES), const),
                  pl.BlockSpec((d, LANES), const),
                  pl.BlockSpec((1, LANES), const),
                  pl.BlockSpec((ROW_TILE, ROW_TILE), const)],
        out_specs=[pl.BlockSpec((ROW_TILE, 1), lambda i: (i, 0)),
                   pl.BlockSpec((ROW_TILE, 1), lambda i: (i, 0)),
                   pl.BlockSpec((1, LANES), const)],
        out_shape=[col, col, jax.ShapeDtypeStruct((1, LANES), F32)],
        scratch_shapes=[pltpu.VMEM((1, LANES), F32)],
        compiler_params=_params("arbitrary"),
        name="moe_router",
    )(h, norm_w.reshape(1, d), whi, wlo, bias, tri)


def _slot_kernel(bucket_ref, rank_ref, start_ref, slot_ref):
    lane = lax.broadcasted_iota(I32, (bucket_ref.shape[0], LANES), 1)
    start = jnp.sum(jnp.where(lane == bucket_ref[...], start_ref[...], 0), axis=-1, keepdims=True)
    slot_ref[...] = start + rank_ref[...]


def _slots(bucket, rank, start):
    n = bucket.shape[0]
    rows = min(n, 8 * ROW_TILE)
    return pl.pallas_call(
        _slot_kernel,
        grid=(n // rows,),
        in_specs=[pl.BlockSpec((rows, 1), lambda i: (i, 0)),
                  pl.BlockSpec((rows, 1), lambda i: (i, 0)),
                  pl.BlockSpec((1, LANES), lambda i: (0, 0))],
        out_specs=pl.BlockSpec((rows, 1), lambda i: (i, 0)),
        out_shape=jax.ShapeDtypeStruct((n, 1), I32),
        compiler_params=_params("parallel"),
        name="moe_slots",
    )(bucket, rank, start)


def _tile_tables(counts, n):
    t = MOE_TILE
    n_tiles = n // t + MOE_BUCKETS
    counts = counts.reshape(LANES).astype(I32)[:MOE_BUCKETS]
    tiles = (counts + t - 1) // t
    tile_end = jnp.cumsum(tiles)
    tile_start = tile_end - tiles
    used = tile_end[-1]
    tile_id = jnp.arange(n_tiles, dtype=I32)
    block = jnp.minimum(tile_id, used - 1)
    tile_bucket = jnp.sum((tile_end[None, :] <= block[:, None]).astype(I32), axis=1)
    onehot = (tile_bucket[:, None] == jnp.arange(MOE_BUCKETS, dtype=I32)[None, :]).astype(I32)
    bucket_count = jnp.sum(onehot * counts[None, :], axis=1)
    bucket_start = jnp.sum(onehot * tile_start[None, :], axis=1)
    rows = jnp.clip(bucket_count - (block - bucket_start) * t, 0, t)
    rows = jnp.where(tile_id < used, rows, 0)
    group = tile_bucket // MOE_PAIRS
    pair = tile_bucket % MOE_PAIRS
    lo_tab, hi_tab = [], []
    for lo in range(MOE_PER_GROUP):
        for hi in range(lo + 1, MOE_PER_GROUP):
            lo_tab.append(lo)
            hi_tab.append(hi)
    pair_hot = (pair[:, None] == jnp.arange(MOE_PAIRS, dtype=I32)[None, :]).astype(I32)
    e_lo = group * MOE_PER_GROUP + jnp.sum(pair_hot * jnp.asarray(lo_tab, I32)[None, :], axis=1)
    e_hi = group * MOE_PER_GROUP + jnp.sum(pair_hot * jnp.asarray(hi_tab, I32)[None, :], axis=1)
    start = jnp.pad(tile_start * t, (0, LANES - MOE_BUCKETS)).reshape(1, LANES)
    return start, (block, rows.astype(I32), group, e_lo, e_hi)


def _dispatch_kernel(slot_ref, h_ref, init_hbm, out_hbm, sem):
    del init_hbm
    rows = h_ref.shape[0]
    base = pl.program_id(0) * rows

    for r in range(rows):
        pltpu.make_async_copy(h_ref.at[r], out_hbm.at[slot_ref[base + r]], sem).start(priority=r % DMA_THREADS)
    pltpu.make_async_copy(h_ref, out_hbm.at[pl.ds(0, rows)], sem).wait()


def _dispatch(h, slot, init):
    n, d = h.shape
    grid_spec = pltpu.PrefetchScalarGridSpec(
        num_scalar_prefetch=1,
        grid=(n // ROW_TILE,),
        in_specs=[pl.BlockSpec((ROW_TILE, d), lambda i, slot: (i, 0)),
                  pl.BlockSpec(memory_space=pl.ANY)],
        out_specs=pl.BlockSpec(memory_space=pl.ANY),
        scratch_shapes=[pltpu.SemaphoreType.DMA(())],
    )
    return pl.pallas_call(
        _dispatch_kernel,
        grid_spec=grid_spec,
        out_shape=jax.ShapeDtypeStruct(init.shape, F32),
        input_output_aliases={2: 0},
        compiler_params=_params("arbitrary"),
        name="moe_dispatch",
    )(slot, h, init)


def _moe_kernel(block_ref, rows_ref, group_ref, elo_ref, ehi_ref,
                x_ref, nw_ref, wr_ref, b_ref, wg1_ref, wu1_ref, wd1_ref, wg2_ref, wu2_ref, wd2_ref,
                o_ref):
    i = pl.program_id(0)

    @pl.when(rows_ref[i] == 0)
    def _():
        o_ref[...] = jnp.zeros_like(o_ref)

    @pl.when(rows_ref[i] > 0)
    def _():
        x = x_ref[...]
        xb = _rms(x, nw_ref[...]).astype(BF16)
        logits = _dot(xb, wr_ref[...]) + b_ref[...]
        lane = lax.broadcasted_iota(I32, logits.shape, 1)

        def pick(idx):
            return jnp.sum(jnp.where(lane == idx, logits, 0.0), axis=-1, keepdims=True)

        lg = pick(group_ref[i])
        gexp = jnp.exp(jnp.where(lane < MOE_GROUPS, logits - lg, -jnp.inf))
        group_w = 1.0 / jnp.sum(gexp, axis=-1, keepdims=True)
        l1 = pick(MOE_GROUPS + elo_ref[i])
        l2 = pick(MOE_GROUPS + ehi_ref[i])
        m = jnp.maximum(l1, l2)
        e1 = jnp.exp(l1 - m)
        e2 = jnp.exp(l2 - m)
        w1 = e1 / (e1 + e2) * group_w
        w2 = e2 / (e1 + e2) * group_w

        hid1 = jax.nn.silu(_dot(xb, wg1_ref[...])) * _dot(xb, wu1_ref[...]) * w1
        hid2 = jax.nn.silu(_dot(xb, wg2_ref[...])) * _dot(xb, wu2_ref[...]) * w2
        o_ref[...] = x + _dot(hid1.astype(BF16), wd1_ref[...]) + _dot(hid2.astype(BF16), wd2_ref[...])


def _moe(xs, norm_w, w_route, bias, w_gate, w_up, w_down, tables):
    n_slots, d = xs.shape
    hidden = w_gate.shape[2]
    const = lambda i, *_: (0, 0)
    tile = lambda i, block, rows, group, elo, ehi: (block[i], 0)
    lo_w = lambda i, block, rows, group, elo, ehi: (elo[i], 0, 0)
    hi_w = lambda i, block, rows, group, elo, ehi: (ehi[i], 0, 0)
    grid_spec = pltpu.PrefetchScalarGridSpec(
        num_scalar_prefetch=5,
        grid=(n_slots // MOE_TILE,),
        in_specs=[pl.BlockSpec((MOE_TILE, d), tile),
                  pl.BlockSpec((1, d), const),
                  pl.BlockSpec((d, LANES), const),
                  pl.BlockSpec((1, LANES), const),
                  pl.BlockSpec((None, d, hidden), lo_w),
                  pl.BlockSpec((None, d, hidden), lo_w),
                  pl.BlockSpec((None, hidden, d), lo_w),
                  pl.BlockSpec((None, d, hidden), hi_w),
                  pl.BlockSpec((None, d, hidden), hi_w),
                  pl.BlockSpec((None, hidden, d), hi_w)],
        out_specs=pl.BlockSpec((MOE_TILE, d), lambda i, *_: (i, 0)),
    )
    return pl.pallas_call(
        _moe_kernel,
        grid_spec=grid_spec,
        out_shape=jax.ShapeDtypeStruct((n_slots, d), F32),
        compiler_params=_params("arbitrary"),
        name="sparse_moe",
    )(*tables, xs, norm_w.reshape(1, d), w_route, bias, w_gate, w_up, w_down, w_gate, w_up, w_down)


def _ple_kernel(slot_ref, ys_hbm, p_ref, nw_ref, wg_ref, wp_ref, post_ref, *rest, final):
    if final:
        o_ref, *bufs, sem = rest
    else:
        win_ref, o_ref, proj_ref, *bufs, sem = rest
    i = pl.program_id(0)
    last = pl.num_programs(0) - 1
    parts = len(bufs)
    part = o_ref.shape[0] // parts
    step_rows = parts * part

    def wait(b):
        pltpu.make_async_copy(ys_hbm.at[pl.ds(0, part)], bufs[b], sem.at[b]).wait()

    @pl.when(i == 0)
    def _():
        for b in range(PLE_AHEAD):
            def issue(r, c, b=b):
                pltpu.make_async_copy(ys_hbm.at[slot_ref[b * part + r]], bufs[b].at[r], sem.at[b]).start()
                return c

            lax.fori_loop(0, part, issue, 0, unroll=DMA_UNROLL)

    def compute(b):
        target = (b + PLE_AHEAD) % parts
        tile = i if b + PLE_AHEAD < parts else jnp.minimum(i + 1, last)
        fetch_base = tile * step_rows + target * part
        groups = 4
        per = part // groups

        def fetch(g):
            for r in range(g * per, (g + 1) * per):
                pltpu.make_async_copy(ys_hbm.at[slot_ref[fetch_base + r]], bufs[target].at[r],
                                      sem.at[target]).start(priority=r % DMA_THREADS)

        rows = slice(b * part, (b + 1) * part)
        d = wg_ref.shape[1]
        x = bufs[b][...]
        xn = _rms(x, nw_ref[...]).astype(BF16)
        fetch(0)
        gate_lo = _dot(xn, wg_ref[:, :d // 2])
        fetch(1)
        gate_hi = _dot(xn, wg_ref[:, d // 2:])
        fetch(2)
        emb = _dot(p_ref[rows, :].astype(BF16), wp_ref[...])
        fetch(3)
        out = x + jax.nn.sigmoid(jnp.concatenate([gate_lo, gate_hi], axis=1)) * emb
        if final:
            o_ref[rows, :] = _rms(out, post_ref[...])
        else:
            o_ref[rows, :] = out
            _project(out, post_ref, win_ref, proj_ref, rows)

    for b in range(parts):
        wait(b)
        compute(b)

    @pl.when(i == last)
    def _():
        for b in range(PLE_AHEAD):
            wait(b)


def _ple(ys, slot, p, norm_w, w_gate, w_proj, post_w, next_w_in=None, split=False):
    n = p.shape[0]
    d = ys.shape[1]
    final = next_w_in is None
    const = lambda i, slot: (0, 0)
    in_specs = [pl.BlockSpec(memory_space=pl.ANY),
                pl.BlockSpec((PLE_TILE, p.shape[1]), lambda i, slot: (i, 0)),
                pl.BlockSpec((1, d), const),
                pl.BlockSpec(w_gate.shape, const),
                pl.BlockSpec(w_proj.shape, const),
                pl.BlockSpec((1, d), const)]
    out_specs = pl.BlockSpec((PLE_TILE, d), lambda i, slot: (i, 0))
    out_shape = jax.ShapeDtypeStruct((n, d), F32)
    operands = [slot, ys, p, norm_w.reshape(1, d), w_gate, w_proj, post_w.reshape(1, d)]
    if not final:
        proj_spec, proj_shape = _proj_out(n, next_w_in.shape[1], split, lambda i, slot: i, PLE_TILE)
        in_specs.append(pl.BlockSpec(next_w_in.shape, const))
        out_specs, out_shape = [out_specs, proj_spec], [out_shape, proj_shape]
        operands.append(next_w_in)
    grid_spec = pltpu.PrefetchScalarGridSpec(
        num_scalar_prefetch=1,
        grid=(n // PLE_TILE,),
        in_specs=in_specs,
        out_specs=out_specs,
        scratch_shapes=[pltpu.VMEM((PLE_TILE // PLE_PARTS, d), F32) for _ in range(PLE_PARTS)]
                       + [pltpu.SemaphoreType.DMA((PLE_PARTS,))],
    )
    return pl.pallas_call(
        functools.partial(_ple_kernel, final=final),
        grid_spec=grid_spec,
        out_shape=out_shape,
        compiler_params=_params("arbitrary"),
        name="per_layer_embedding",
    )(*operands)


def kernel(x, p, attn_norm_w, ffn_norm_w, final_norm_w, even_w_in, even_w_out, ret_norm_w, sg_norm_w, sg_spatial_w, sg_spatial_b, odd_w_in, odd_w_out, moe_w_group, moe_b_group, moe_w_expert, moe_b_expert, moe_w_gate, moe_w_up, moe_w_down, ple_norm_w, ple_w_gate, ple_w_proj):
    batch, seq, d = x.shape
    depth = p.shape[0]
    n = batch * seq
    n_slots = n + MOE_BUCKETS * MOE_TILE

    def w_in(layer):
        odd = layer % 2 == 1
        return (odd_w_in if odd else even_w_in)[layer // 2].astype(BF16), odd

    h = x.reshape(n, d)
    sorted_rows = jnp.zeros((n_slots, d), F32)
    proj = _norm_matmul(h, attn_norm_w[0], *w_in(0))
    for i in range(depth):
        j = i // 2
        if i % 2 == 0:
            h = _even_mixer(h, proj, batch, seq, even_w_out[j].astype(BF16), ret_norm_w[j], sg_norm_w[j],
                            sg_spatial_w[j], sg_spatial_b[j])
        else:
            att = _stick_breaking(proj, batch, seq)
            h = _matmul_residual(att, odd_w_out[j].astype(BF16), h)

        w_route = jnp.concatenate([moe_w_group[i], moe_w_expert[i]], axis=1)
        w_route = jnp.pad(w_route, ((0, 0), (0, LANES - w_route.shape[1])))
        whi, wlo = _split_bf16(w_route)
        bias = jnp.concatenate([moe_b_group[i], moe_b_expert[i]])
        bias = jnp.pad(bias, (0, LANES - bias.shape[0])).reshape(1, LANES)
        bucket, rank, counts = _router(h, ffn_norm_w[i], whi, wlo, bias)
        start, tables = _tile_tables(counts, n)
        slot = _slots(bucket, rank, start).reshape(n)
        xs = _dispatch(h, slot, sorted_rows)
        sorted_rows = _moe(xs, ffn_norm_w[i], whi, bias, moe_w_gate[i].astype(BF16), moe_w_up[i].astype(BF16),
                           moe_w_down[i].astype(BF16), tables)
        ple = (sorted_rows, slot, p[i].reshape(n, -1), ple_norm_w[i], ple_w_gate[i].astype(BF16),
               ple_w_proj[i].astype(BF16))
        if i == depth - 1:
            h = _ple(*ple, final_norm_w)
        else:
            h, proj = _ple(*ple, attn_norm_w[i + 1], *w_in(i + 1))
    return h.reshape(batch, seq, d)
```

```python
import functools

import jax
import jax.numpy as jnp
from jax import lax
from jax.experimental import pallas as pl
from jax.experimental.pallas import tpu as pltpu

F32 = jnp.float32
BF16 = jnp.bfloat16
I32 = jnp.int32

LANES = 128
CHUNK = 128
RET_HEADS = 4
RET_DIM = 128
RET_WIDTH = RET_HEADS * RET_DIM
SG_GROUPS = 4
SG_DIM = 128
SG_WIDTH = SG_GROUPS * SG_DIM
SB_DIM = 64
MOE_GROUPS = 4
MOE_PER_GROUP = 8
MOE_EXPERTS = MOE_GROUPS * MOE_PER_GROUP
MOE_PAIRS = MOE_PER_GROUP * (MOE_PER_GROUP - 1) // 2
MOE_BUCKETS = MOE_GROUPS * MOE_PAIRS
ROPE_BASE = 10000.0
EPS = 1e-6

ROW_TILE = 512
MIX_TILE = 512
PROJ_CHUNK = 512
PLE_TILE = 1024
PLE_PARTS = 4
PLE_AHEAD = 2
MOE_TILE = 256
ATT_Q = 256
EXP_UNDERFLOW = -104.0
DMA_UNROLL = 8
DMA_THREADS = 2
VMEM_LIMIT = 56 * 1024 * 1024


def _params(*sem):
    return pltpu.CompilerParams(dimension_semantics=sem, vmem_limit_bytes=VMEM_LIMIT)


def _rms(x, w):
    return x * lax.rsqrt(jnp.mean(x * x, axis=-1, keepdims=True) + EPS) * w


def _dot(a, b):
    return jnp.dot(a, b, preferred_element_type=F32)


def _dot_nt(a, b):
    return lax.dot_general(a, b, (((1,), (1,)), ((), ())), preferred_element_type=F32)


def _dot_tn(a, b):
    return lax.dot_general(a, b, (((0,), (0,)), ((), ())), preferred_element_type=F32)


def _aligned(index, multiple):
    return index if isinstance(index, int) else pl.multiple_of(index, multiple)


def _split_bf16(x):
    hi = x.astype(BF16)
    lo = (x - hi.astype(F32)).astype(BF16)
    return hi, lo


def _project(x, nw_ref, w_ref, o_ref, rows=slice(None)):
    xn = _rms(x, nw_ref[...]).astype(BF16)
    for c in range(0, w_ref.shape[1], PROJ_CHUNK):
        res = _dot(xn, w_ref[:, c:c + PROJ_CHUNK]).astype(o_ref.dtype)
        if len(o_ref.shape) == 3:
            for s in range(PROJ_CHUNK // LANES):
                o_ref[c // LANES + s, rows, :] = res[:, s * LANES:(s + 1) * LANES]
        else:
            o_ref[rows, c:c + PROJ_CHUNK] = res


def _proj_out(n, m, split, index, rows=ROW_TILE):
    if split:
        return (pl.BlockSpec((m // LANES, rows, LANES), lambda *a: (0, index(*a), 0)),
                jax.ShapeDtypeStruct((m // LANES, n, LANES), BF16))
    return (pl.BlockSpec((rows, m), lambda *a: (index(*a), 0)), jax.ShapeDtypeStruct((n, m), BF16))


def _norm_matmul_kernel(h_ref, nw_ref, w_ref, o_ref):
    _project(h_ref[...], nw_ref, w_ref, o_ref)


def _norm_matmul(h, norm_w, w, split):
    n, d = h.shape
    out_spec, out_shape = _proj_out(n, w.shape[1], split, lambda i: i)
    return pl.pallas_call(
        _norm_matmul_kernel,
        grid=(n // ROW_TILE,),
        in_specs=[pl.BlockSpec((ROW_TILE, d), lambda i: (i, 0)),
                  pl.BlockSpec((1, d), lambda i: (0, 0)),
                  pl.BlockSpec(w.shape, lambda i: (0, 0))],
        out_specs=out_spec,
        out_shape=out_shape,
        compiler_params=_params("parallel"),
        name="norm_matmul",
    )(h, norm_w.reshape(1, d), w)


def _matmul_residual_kernel(a_ref, w_ref, h_ref, o_ref):
    a = jnp.concatenate([a_ref[c] for c in range(a_ref.shape[0])], axis=1)
    o_ref[...] = h_ref[...] + _dot(a, w_ref[...])


def _matmul_residual(a, w, h):
    groups, n, _ = a.shape
    d = w.shape[1]
    return pl.pallas_call(
        _matmul_residual_kernel,
        grid=(n // ROW_TILE,),
        in_specs=[pl.BlockSpec((groups, ROW_TILE, LANES), lambda i: (0, i, 0)),
                  pl.BlockSpec(w.shape, lambda i: (0, 0)),
                  pl.BlockSpec((ROW_TILE, d), lambda i: (i, 0))],
        out_specs=pl.BlockSpec((ROW_TILE, d), lambda i: (i, 0)),
        out_shape=jax.ShapeDtypeStruct((n, d), F32),
        compiler_params=_params("parallel"),
        name="matmul_residual",
    )(a, w, h)


def _even_mixer_kernel(h_ref, proj_ref, cos_ref, sin_ref, intra_ref, qdec_ref, kdec_ref, cdec_ref,
                       retw_ref, sgnw_ref, sgw_ref, sgb_ref, wout_ref, o_ref, state_ref, mixed_ref):
    @pl.when(pl.program_id(1) == 0)
    def _():
        state_ref[...] = jnp.zeros_like(state_ref)

    def chunk(c, carry):
        r0 = pl.multiple_of(c * CHUNK, CHUNK)
        rows = pl.ds(r0, CHUNK)
        cos = cos_ref[rows, :]
        sin = sin_ref[rows, :]

        def rope(t):
            return t * cos + pltpu.roll(t, RET_DIM // 2, 1) * sin

        heads = range(RET_HEADS)
        lanes = [slice(hd * RET_DIM, (hd + 1) * RET_DIM) for hd in heads]

        def field(idx, hd):
            return proj_ref[rows, idx * RET_WIDTH + hd * RET_DIM:idx * RET_WIDTH + (hd + 1) * RET_DIM]

        q = [rope(field(0, hd).astype(F32)) for hd in heads]
        k = [rope(field(1, hd).astype(F32)) for hd in heads]
        v = [field(2, hd) for hd in heads]
        scores = [_dot_nt(q[hd].astype(BF16), k[hd].astype(BF16)) * intra_ref[hd] for hd in heads]
        st = [state_ref[hd] for hd in heads]
        out = [_dot(scores[hd].astype(BF16), v[hd]) + _dot((q[hd] * qdec_ref[hd]).astype(BF16), st[hd].astype(BF16))
               for hd in heads]
        for hd in heads:
            state_ref[hd] = st[hd] * cdec_ref[hd] + _dot_tn((k[hd] * kdec_ref[hd]).astype(BF16), v[hd])
        for hd in heads:
            mu = jnp.mean(out[hd], axis=-1, keepdims=True)
            cen = out[hd] - mu
            var = jnp.mean(cen * cen, axis=-1, keepdims=True)
            ret = cen * lax.rsqrt(var + EPS) * retw_ref[:, lanes[hd]]
            mixed_ref[rows, lanes[hd]] = (jax.nn.silu(field(3, hd).astype(F32)) * ret).astype(BF16)

        base = 4 * RET_WIDTH
        gv = jax.nn.gelu(proj_ref[rows, base + SG_WIDTH:base + 2 * SG_WIDTH].astype(F32))
        vsn = _rms(gv, sgnw_ref[...]).astype(BF16)
        for gi in range(SG_GROUPS):
            lanes = slice(gi * SG_DIM, (gi + 1) * SG_DIM)
            u = proj_ref[rows, base + gi * SG_DIM:base + (gi + 1) * SG_DIM].astype(F32)
            mix = _dot(sgw_ref[gi], vsn[:, lanes]) + sgb_ref[gi]
            mixed_ref[rows, RET_WIDTH + gi * SG_DIM:RET_WIDTH + (gi + 1) * SG_DIM] = (
                jax.nn.gelu(u) * mix).astype(BF16)
        return carry

    lax.fori_loop(0, h_ref.shape[0] // CHUNK, chunk, 0)
    o_ref[...] = h_ref[...] + _dot(mixed_ref[...], wout_ref[...])


def _even_mixer(h, proj, batch, seq, w_out, ret_norm_w, sg_norm_w, sg_w, sg_b):
    n, d = h.shape
    steps = seq // MIX_TILE
    inv_freq = ROPE_BASE ** (-jnp.arange(0, RET_DIM, 2, dtype=F32) / RET_DIM)
    ang = jnp.arange(seq, dtype=F32)[:, None] * inv_freq[None, :]
    cos = jnp.concatenate([jnp.cos(ang), jnp.cos(ang)], axis=-1)
    sin = jnp.concatenate([-jnp.sin(ang), jnp.sin(ang)], axis=-1)
    log_gamma = jnp.log1p(-jnp.exp2(-5.0 - jnp.arange(RET_HEADS, dtype=F32)))
    pos = jnp.arange(CHUNK, dtype=F32)
    diff = pos[:, None] - pos[None, :]
    scale = RET_DIM ** -0.5
    intra = jnp.where(diff >= 0, jnp.exp(log_gamma[:, None, None] * jnp.maximum(diff, 0.0)), 0.0) * scale
    bshape = (RET_HEADS, CHUNK, RET_DIM)
    qdec = jnp.broadcast_to(jnp.exp(log_gamma[:, None] * (pos[None, :] + 1.0))[:, :, None], bshape)
    kdec = jnp.broadcast_to(jnp.exp(log_gamma[:, None] * (CHUNK - 1.0 - pos[None, :]))[:, :, None] * scale, bshape)
    cdec = jnp.broadcast_to(jnp.exp(log_gamma * CHUNK)[:, None, None], bshape)
    tril = jnp.tril(jnp.ones((CHUNK, CHUNK), dtype=bool))
    sgw = jnp.where(tril, sg_w, jnp.zeros_like(sg_w)).astype(BF16)
    sgb = jnp.broadcast_to(sg_b[:, :, None], (SG_GROUPS, CHUNK, SG_DIM))

    const3 = lambda b, s: (0, 0, 0)
    const2 = lambda b, s: (0, 0)
    row = lambda b, s: (b * steps + s, 0)
    return pl.pallas_call(
        _even_mixer_kernel,
        grid=(batch, steps),
        in_specs=[pl.BlockSpec((MIX_TILE, d), row),
                  pl.BlockSpec((MIX_TILE, proj.shape[1]), row),
                  pl.BlockSpec((MIX_TILE, RET_DIM), lambda b, s: (s, 0)),
                  pl.BlockSpec((MIX_TILE, RET_DIM), lambda b, s: (s, 0)),
                  pl.BlockSpec(bshape, const3),
                  pl.BlockSpec(bshape, const3),
                  pl.BlockSpec(bshape, const3),
                  pl.BlockSpec(bshape, const3),
                  pl.BlockSpec((1, RET_WIDTH), const2),
                  pl.BlockSpec((1, SG_WIDTH), const2),
                  pl.BlockSpec((SG_GROUPS, CHUNK, CHUNK), const3),
                  pl.BlockSpec((SG_GROUPS, CHUNK, SG_DIM), const3),
                  pl.BlockSpec(w_out.shape, const2)],
        out_specs=pl.BlockSpec((MIX_TILE, d), row),
        out_shape=jax.ShapeDtypeStruct((n, d), F32),
        scratch_shapes=[pltpu.VMEM((RET_HEADS, RET_DIM, RET_DIM), F32),
                        pltpu.VMEM((MIX_TILE, RET_WIDTH + SG_WIDTH), BF16)],
        compiler_params=_params("parallel", "arbitrary"),
        name="even_mixer",
    )(h, proj, cos, sin, intra, qdec, kdec, cdec, ret_norm_w.reshape(1, -1), sg_norm_w.reshape(1, -1),
      sgw, sgb, w_out)


def _stick_kernel(q_ref, k_ref, v_ref, cum_ref, o_ref, qs_ref):
    seq = q_ref.shape[0]
    n_sub = ATT_Q // CHUNK
    row = lax.broadcasted_iota(I32, (CHUNK, CHUNK), 0)
    lane = lax.broadcasted_iota(I32, (CHUNK, CHUNK), 1)
    tri = lane < row
    first_k = lane < SB_DIM
    first_q = lax.broadcasted_iota(I32, (ATT_Q, LANES), 1) < SB_DIM
    cum = cum_ref[...]

    def run(jobs, carry, acc):
        stacked = []
        for z, _, _, diagonal, _, _ in jobs:
            stay = jnp.minimum(-z, 0.0) - jnp.log(1.0 + jnp.exp(-jnp.abs(z)))
            if diagonal:
                stay = jnp.where(tri, stay, 0.0)
            stacked.append(stay.astype(BF16))
        sums = [_dot(lhs, cum) for lhs in stacked]
        weights = []
        for (z, s, head, diagonal, _, live), both in zip(jobs, sums):
            c = head * n_sub + s
            a = jnp.exp(z + carry[c] + both[:, :CHUNK])
            block_sum = both[:, CHUNK:]
            if diagonal:
                a = jnp.where(tri, a, 0.0)
            if live is not None:
                a = a * live
                block_sum = block_sum * live
            carry[c] = carry[c] + block_sum
            weights.append(a.astype(BF16))
        for (_, s, _, _, v_head, _), a in zip(jobs, weights):
            acc[s] = acc[s] + _dot(a, v_head)

    def load_kv(j):
        keys = pl.ds(_aligned(j * CHUNK, CHUNK), CHUNK)
        kb = k_ref[keys, :]
        vb = v_ref[keys, :]
        zero = jnp.zeros_like(vb)
        return kb, (jnp.where(first_k, vb, zero), jnp.where(first_k, zero, vb))

    def q_rows(s, head):
        r = head * ATT_Q + s * CHUNK
        return slice(r, r + CHUNK)

    def tile(t, first):
        r0 = _aligned(t * ATT_Q, ATT_Q)
        q = q_ref[pl.ds(r0, ATT_Q), :] * (SB_DIM ** -0.5)
        zero = jnp.zeros_like(q)
        qs_ref[0:ATT_Q, :] = jnp.where(first_q, q, zero)
        qs_ref[ATT_Q:, :] = jnp.where(first_q, zero, q)
        blank = jnp.zeros((CHUNK, CHUNK), F32)
        carry = [blank] * (2 * n_sub)
        acc = [blank] * n_sub

        jobs = []
        for jd in reversed(range(n_sub)):
            kb, v_heads = load_kv(t * n_sub + jd)
            for s in range(jd, n_sub):
                for head in range(2):
                    jobs.append((_dot_nt(qs_ref[q_rows(s, head), :], kb), s, head, s == jd, v_heads[head], None))
        if not first:
            for b in range(n_sub):
                kb, v_heads = load_kv(t * n_sub - 1 - b)
                z_all = _dot_nt(qs_ref[...], kb)
                for s in range(n_sub - b):
                    for head in range(2):
                        jobs.append((z_all[q_rows(s, head), :], s, head, False, v_heads[head], None))
        run(jobs, carry, acc)

        def top_carry(carry):
            top = carry[0]
            for other in carry[1:]:
                top = jnp.maximum(top, other)
            return jnp.max(top)

        def older(state):
            trip, _, carry, acc = state
            carry, acc = list(carry), list(acc)
            jobs = []
            for s in range(n_sub):
                j = (t - 1) * n_sub + s - 1 - trip
                live = (j >= 0).astype(F32)
                kb, v_heads = load_kv(jnp.maximum(j, 0))
                for head in range(2):
                    jobs.append((_dot_nt(qs_ref[q_rows(s, head), :], kb), s, head, False, v_heads[head], live))
            run(jobs, carry, acc)
            return trip + 1, top_carry(carry), tuple(carry), tuple(acc)

        def unfinished(state):
            return (state[0] < t * n_sub - 1) & (state[1] > EXP_UNDERFLOW)

        if not first:
            start = (jnp.int32(0), top_carry(carry), tuple(carry), tuple(acc))
            acc = lax.while_loop(unfinished, older, start)[3]
        for s in range(n_sub):
            o_ref[pl.ds(r0 + s * CHUNK, CHUNK), :] = acc[s].astype(o_ref.dtype)

    tile(0, True)

    def later_tile(t, c):
        tile(t, False)
        return c

    lax.fori_loop(1, seq // ATT_Q, later_tile, 0)


def _stick_breaking(qkv, batch, seq):
    pairs = qkv.shape[0] // 3
    n = qkv.shape[1]
    tri = (jnp.arange(CHUNK)[:, None] >= jnp.arange(CHUNK)[None, :])
    cum = jnp.concatenate([tri.astype(BF16), jnp.ones((CHUNK, CHUNK), BF16)], axis=1)
    blk = (None, seq, LANES)
    return pl.pallas_call(
        _stick_kernel,
        grid=(batch, pairs),
        in_specs=[pl.BlockSpec(blk, lambda b, p: (p, b, 0)),
                  pl.BlockSpec(blk, lambda b, p: (pairs + p, b, 0)),
                  pl.BlockSpec(blk, lambda b, p: (2 * pairs + p, b, 0)),
                  pl.BlockSpec((CHUNK, 2 * CHUNK), lambda b, p: (0, 0))],
        out_specs=pl.BlockSpec(blk, lambda b, p: (p, b, 0)),
        out_shape=jax.ShapeDtypeStruct((pairs, n, LANES), BF16),
        scratch_shapes=[pltpu.VMEM((2 * ATT_Q, LANES), BF16)],
        compiler_params=_params("parallel", "parallel"),
        name="stick_breaking",
    )(qkv, qkv, qkv, cum)


def _router_logits(xn, whi_ref, wlo_ref, b_ref):
    hi, lo = _split_bf16(xn)
    return _dot(hi, whi_ref[...]) + _dot(hi, wlo_ref[...]) + _dot(lo, whi_ref[...]) + b_ref[...]


def _router_kernel(h_ref, nw_ref, whi_ref, wlo_ref, b_ref, tri_ref, bucket_ref, rank_ref, counts_ref, run_ref):
    @pl.when(pl.program_id(0) == 0)
    def _():
        run_ref[...] = jnp.zeros_like(run_ref)

    logits = _router_logits(_rms(h_ref[...], nw_ref[...]), whi_ref, wlo_ref, b_ref)
    lane = lax.broadcasted_iota(I32, logits.shape, 1)
    neg = jnp.float32(-jnp.inf)

    def first_max(vals):
        m = jnp.max(vals, axis=-1, keepdims=True)
        return jnp.min(jnp.where(vals == m, lane, LANES), axis=-1, keepdims=True)

    group = first_max(jnp.where(lane < MOE_GROUPS, logits, neg))
    rel = lane - MOE_GROUPS
    in_group = (rel >= 0) & (rel < MOE_EXPERTS) & ((rel >> 3) == group)
    ev = jnp.where(in_group, logits, neg)
    i1 = first_max(ev)
    i2 = first_max(jnp.where(lane == i1, neg, ev))
    a = (i1 - MOE_GROUPS) & (MOE_PER_GROUP - 1)
    b = (i2 - MOE_GROUPS) & (MOE_PER_GROUP - 1)
    lo = jnp.minimum(a, b)
    hi = jnp.maximum(a, b)
    pair = (lo * (2 * MOE_PER_GROUP - 1 - lo)) // 2 + (hi - lo - 1)
    bucket = group * MOE_PAIRS + pair

    onehot = lane == bucket
    before = _dot(tri_ref[...], onehot.astype(BF16)) + run_ref[...]
    rank = jnp.sum(jnp.where(onehot, before, 0.0), axis=-1, keepdims=True)
    run_ref[...] += jnp.sum(onehot.astype(F32), axis=0, keepdims=True)
    bucket_ref[...] = bucket
    rank_ref[...] = rank.astype(I32)
    counts_ref[...] = run_ref[...]


def _router(h, norm_w, whi, wlo, bias):
    n, d = h.shape
    const = lambda i: (0, 0)
    tri = (jnp.arange(ROW_TILE)[:, None] > jnp.arange(ROW_TILE)[None, :]).astype(BF16)
    col = jax.ShapeDtypeStruct((n, 1), I32)
    return pl.pallas_call(
        _router_kernel,
        grid=(n // ROW_TILE,),
        in_specs=[pl.BlockSpec((ROW_TILE, d), lambda i: (i, 0)),
                  pl.BlockSpec((1, d), const),
                  pl.BlockSpec((d, LANES), const),
                  pl.BlockSpec((d, LANES), const),
                  pl.BlockSpec((1, LANES), const),
                  pl.BlockSpec((ROW_TILE, ROW_TILE), const)],
        out_specs=[pl.BlockSpec((ROW_TILE, 1), lambda i: (i, 0)),
                   pl.BlockSpec((ROW_TILE, 1), lambda i: (i, 0)),
                   pl.BlockSpec((1, LANES), const)],
        out_shape=[col, col, jax.ShapeDtypeStruct((1, LANES), F32)],
        scratch_shapes=[pltpu.VMEM((1, LANES), F32)],
        compiler_params=_params("arbitrary"),
        name="moe_router",
    )(h, norm_w.reshape(1, d), whi, wlo, bias, tri)


def _slot_kernel(bucket_ref, rank_ref, start_ref, slot_ref):
    lane = lax.broadcasted_iota(I32, (bucket_ref.shape[0], LANES), 1)
    start = jnp.sum(jnp.where(lane == bucket_ref[...], start_ref[...], 0), axis=-1, keepdims=True)
    slot_ref[...] = start + rank_ref[...]


def _slots(bucket, rank, start):
    n = bucket.shape[0]
    rows = min(n, 8 * ROW_TILE)
    return pl.pallas_call(
        _slot_kernel,
        grid=(n // rows,),
        in_specs=[pl.BlockSpec((rows, 1), lambda i: (i, 0)),
                  pl.BlockSpec((rows, 1), lambda i: (i, 0)),
                  pl.BlockSpec((1, LANES), lambda i: (0, 0))],
        out_specs=pl.BlockSpec((rows, 1), lambda i: (i, 0)),
        out_shape=jax.ShapeDtypeStruct((n, 1), I32),
        compiler_params=_params("parallel"),
        name="moe_slots",
    )(bucket, rank, start)


def _tile_tables(counts, n):
    t = MOE_TILE
    n_tiles = n // t + MOE_BUCKETS
    counts = counts.reshape(LANES).astype(I32)[:MOE_BUCKETS]
    tiles = (counts + t - 1) // t
    tile_end = jnp.cumsum(tiles)
    tile_start = tile_end - tiles
    used = tile_end[-1]
    tile_id = jnp.arange(n_tiles, dtype=I32)
    block = jnp.minimum(tile_id, used - 1)
    tile_bucket = jnp.sum((tile_end[None, :] <= block[:, None]).astype(I32), axis=1)
    onehot = (tile_bucket[:, None] == jnp.arange(MOE_BUCKETS, dtype=I32)[None, :]).astype(I32)
    bucket_count = jnp.sum(onehot * counts[None, :], axis=1)
    bucket_start = jnp.sum(onehot * tile_start[None, :], axis=1)
    rows = jnp.clip(bucket_count - (block - bucket_start) * t, 0, t)
    rows = jnp.where(tile_id < used, rows, 0)
    group = tile_bucket // MOE_PAIRS
    pair = tile_bucket % MOE_PAIRS
    lo_tab, hi_tab = [], []
    for lo in range(MOE_PER_GROUP):
        for hi in range(lo + 1, MOE_PER_GROUP):
            lo_tab.append(lo)
            hi_tab.append(hi)
    pair_hot = (pair[:, None] == jnp.arange(MOE_PAIRS, dtype=I32)[None, :]).astype(I32)
    e_lo = group * MOE_PER_GROUP + jnp.sum(pair_hot * jnp.asarray(lo_tab, I32)[None, :], axis=1)
    e_hi = group * MOE_PER_GROUP + jnp.sum(pair_hot * jnp.asarray(hi_tab, I32)[None, :], axis=1)
    start = jnp.pad(tile_start * t, (0, LANES - MOE_BUCKETS)).reshape(1, LANES)
    return start, (block, rows.astype(I32), group, e_lo, e_hi)


def _dispatch_kernel(slot_ref, h_ref, init_hbm, out_hbm, sem):
    del init_hbm
    rows = h_ref.shape[0]
    base = pl.program_id(0) * rows

    for r in range(rows):
        pltpu.make_async_copy(h_ref.at[r], out_hbm.at[slot_ref[base + r]], sem).start(priority=r % DMA_THREADS)
    pltpu.make_async_copy(h_ref, out_hbm.at[pl.ds(0, rows)], sem).wait()


def _dispatch(h, slot, init):
    n, d = h.shape
    grid_spec = pltpu.PrefetchScalarGridSpec(
        num_scalar_prefetch=1,
        grid=(n // ROW_TILE,),
        in_specs=[pl.BlockSpec((ROW_TILE, d), lambda i, slot: (i, 0)),
                  pl.BlockSpec(memory_space=pl.ANY)],
        out_specs=pl.BlockSpec(memory_space=pl.ANY),
        scratch_shapes=[pltpu.SemaphoreType.DMA(())],
    )
    return pl.pallas_call(
        _dispatch_kernel,
        grid_spec=grid_spec,
        out_shape=jax.ShapeDtypeStruct(init.shape, F32),
        input_output_aliases={2: 0},
        compiler_params=_params("arbitrary"),
        name="moe_dispatch",
    )(slot, h, init)


def _moe_kernel(block_ref, rows_ref, group_ref, elo_ref, ehi_ref,
                x_ref, nw_ref, wr_ref, b_ref, wg1_ref, wu1_ref, wd1_ref, wg2_ref, wu2_ref, wd2_ref,
                o_ref):
    i = pl.program_id(0)

    @pl.when(rows_ref[i] == 0)
    def _():
        o_ref[...] = jnp.zeros_like(o_ref)

    @pl.when(rows_ref[i] > 0)
    def _():
        x = x_ref[...]
        xb = _rms(x, nw_ref[...]).astype(BF16)
        logits = _dot(xb, wr_ref[...]) + b_ref[...]
        lane = lax.broadcasted_iota(I32, logits.shape, 1)

        def pick(idx):
            return jnp.sum(jnp.where(lane == idx, logits, 0.0), axis=-1, keepdims=True)

        lg = pick(group_ref[i])
        gexp = jnp.exp(jnp.where(lane < MOE_GROUPS, logits - lg, -jnp.inf))
        group_w = 1.0 / jnp.sum(gexp, axis=-1, keepdims=True)
        l1 = pick(MOE_GROUPS + elo_ref[i])
        l2 = pick(MOE_GROUPS + ehi_ref[i])
        m = jnp.maximum(l1, l2)
        e1 = jnp.exp(l1 - m)
        e2 = jnp.exp(l2 - m)
        w1 = e1 / (e1 + e2) * group_w
        w2 = e2 / (e1 + e2) * group_w

        hid1 = jax.nn.silu(_dot(xb, wg1_ref[...])) * _dot(xb, wu1_ref[...]) * w1
        hid2 = jax.nn.silu(_dot(xb, wg2_ref[...])) * _dot(xb, wu2_ref[...]) * w2
        o_ref[...] = x + _dot(hid1.astype(BF16), wd1_ref[...]) + _dot(hid2.astype(BF16), wd2_ref[...])


def _moe(xs, norm_w, w_route, bias, w_gate, w_up, w_down, layer, tables):
    n_slots, d = xs.shape
    hidden = w_gate.shape[3]
    const = lambda i, *_: (0, 0)
    tile = lambda i, block, rows, group, elo, ehi: (block[i], 0)
    lo_w = lambda i, block, rows, group, elo, ehi: (layer, elo[i], 0, 0)
    hi_w = lambda i, block, rows, group, elo, ehi: (layer, ehi[i], 0, 0)
    grid_spec = pltpu.PrefetchScalarGridSpec(
        num_scalar_prefetch=5,
        grid=(n_slots // MOE_TILE,),
        in_specs=[pl.BlockSpec((MOE_TILE, d), tile),
                  pl.BlockSpec((1, d), const),
                  pl.BlockSpec((d, LANES), const),
                  pl.BlockSpec((1, LANES), const),
                  pl.BlockSpec((None, None, d, hidden), lo_w),
                  pl.BlockSpec((None, None, d, hidden), lo_w),
                  pl.BlockSpec((None, None, hidden, d), lo_w),
                  pl.BlockSpec((None, None, d, hidden), hi_w),
                  pl.BlockSpec((None, None, d, hidden), hi_w),
                  pl.BlockSpec((None, None, hidden, d), hi_w)],
        out_specs=pl.BlockSpec((MOE_TILE, d), lambda i, *_: (i, 0)),
    )
    return pl.pallas_call(
        _moe_kernel,
        grid_spec=grid_spec,
        out_shape=jax.ShapeDtypeStruct((n_slots, d), F32),
        compiler_params=_params("arbitrary"),
        name="sparse_moe",
    )(*tables, xs, norm_w.reshape(1, d), w_route, bias, w_gate, w_up, w_down, w_gate, w_up, w_down)


def _ple_kernel(slot_ref, ys_hbm, p_ref, nw_ref, wg_ref, wp_ref, post_ref, *rest, final):
    if final:
        o_ref, *bufs, sem = rest
    else:
        win_ref, o_ref, proj_ref, *bufs, sem = rest
    i = pl.program_id(0)
    last = pl.num_programs(0) - 1
    parts = len(bufs)
    part = o_ref.shape[0] // parts
    step_rows = parts * part

    def wait(b):
        pltpu.make_async_copy(ys_hbm.at[pl.ds(0, part)], bufs[b], sem.at[b]).wait()

    @pl.when(i == 0)
    def _():
        for b in range(PLE_AHEAD):
            def issue(r, c, b=b):
                pltpu.make_async_copy(ys_hbm.at[slot_ref[b * part + r]], bufs[b].at[r], sem.at[b]).start()
                return c

            lax.fori_loop(0, part, issue, 0, unroll=DMA_UNROLL)

    def compute(b):
        target = (b + PLE_AHEAD) % parts
        tile = i if b + PLE_AHEAD < parts else jnp.minimum(i + 1, last)
        fetch_base = tile * step_rows + target * part
        groups = 4
        per = part // groups

        def fetch(g):
            for r in range(g * per, (g + 1) * per):
                pltpu.make_async_copy(ys_hbm.at[slot_ref[fetch_base + r]], bufs[target].at[r],
                                      sem.at[target]).start(priority=r % DMA_THREADS)

        rows = slice(b * part, (b + 1) * part)
        d = wg_ref.shape[1]
        x = bufs[b][...]
        xn = _rms(x, nw_ref[...]).astype(BF16)
        fetch(0)
        gate_lo = _dot(xn, wg_ref[:, :d // 2])
        fetch(1)
        gate_hi = _dot(xn, wg_ref[:, d // 2:])
        fetch(2)
        emb = _dot(p_ref[rows, :].astype(BF16), wp_ref[...])
        fetch(3)
        out = x + jax.nn.sigmoid(jnp.concatenate([gate_lo, gate_hi], axis=1)) * emb
        if final:
            o_ref[rows, :] = _rms(out, post_ref[...])
        else:
            o_ref[rows, :] = out
            _project(out, post_ref, win_ref, proj_ref, rows)

    for b in range(parts):
        wait(b)
        compute(b)

    @pl.when(i == last)
    def _():
        for b in range(PLE_AHEAD):
            wait(b)


def _ple(ys, slot, p, layer, norm_w, w_gate, w_proj, post_w, next_w_in=None, split=False):
    n = slot.shape[0]
    d = ys.shape[1]
    steps = n // PLE_TILE
    final = next_w_in is None
    const = lambda i, slot: (0, 0)
    in_specs = [pl.BlockSpec(memory_space=pl.ANY),
                pl.BlockSpec((PLE_TILE, p.shape[1]), lambda i, slot: (layer * steps + i, 0)),
                pl.BlockSpec((1, d), const),
                pl.BlockSpec(w_gate.shape, const),
                pl.BlockSpec(w_proj.shape, const),
                pl.BlockSpec((1, d), const)]
    out_specs = pl.BlockSpec((PLE_TILE, d), lambda i, slot: (i, 0))
    out_shape = jax.ShapeDtypeStruct((n, d), F32)
    operands = [slot, ys, p, norm_w.reshape(1, d), w_gate, w_proj, post_w.reshape(1, d)]
    if not final:
        proj_spec, proj_shape = _proj_out(n, next_w_in.shape[1], split, lambda i, slot: i, PLE_TILE)
        in_specs.append(pl.BlockSpec(next_w_in.shape, const))
        out_specs, out_shape = [out_specs, proj_spec], [out_shape, proj_shape]
        operands.append(next_w_in)
    grid_spec = pltpu.PrefetchScalarGridSpec(
        num_scalar_prefetch=1,
        grid=(n // PLE_TILE,),
        in_specs=in_specs,
        out_specs=out_specs,
        scratch_shapes=[pltpu.VMEM((PLE_TILE // PLE_PARTS, d), F32) for _ in range(PLE_PARTS)]
                       + [pltpu.SemaphoreType.DMA((PLE_PARTS,))],
    )
    return pl.pallas_call(
        functools.partial(_ple_kernel, final=final),
        grid_spec=grid_spec,
        out_shape=out_shape,
        compiler_params=_params("arbitrary"),
        name="per_layer_embedding",
    )(*operands)


def kernel(x, p, attn_norm_w, ffn_norm_w, final_norm_w, even_w_in, even_w_out, ret_norm_w, sg_norm_w, sg_spatial_w, sg_spatial_b, odd_w_in, odd_w_out, moe_w_group, moe_b_group, moe_w_expert, moe_b_expert, moe_w_gate, moe_w_up, moe_w_down, ple_norm_w, ple_w_gate, ple_w_proj):
    batch, seq, d = x.shape
    depth = p.shape[0]
    n = batch * seq
    n_slots = n + MOE_BUCKETS * MOE_TILE

    def w_in(layer):
        odd = layer % 2 == 1
        return (odd_w_in if odd else even_w_in)[layer // 2].astype(BF16), odd

    h = x.reshape(n, d)
    p_rows = p.reshape(depth * n, p.shape[-1])
    expert_w = (moe_w_gate.astype(BF16), moe_w_up.astype(BF16), moe_w_down.astype(BF16))
    sorted_rows = jnp.zeros((n_slots, d), F32)
    proj = _norm_matmul(h, attn_norm_w[0], *w_in(0))
    for i in range(depth):
        j = i // 2
        if i % 2 == 0:
            h = _even_mixer(h, proj, batch, seq, even_w_out[j].astype(BF16), ret_norm_w[j], sg_norm_w[j],
                            sg_spatial_w[j], sg_spatial_b[j])
        else:
            att = _stick_breaking(proj, batch, seq)
            h = _matmul_residual(att, odd_w_out[j].astype(BF16), h)

        w_route = jnp.concatenate([moe_w_group[i], moe_w_expert[i]], axis=1)
        w_route = jnp.pad(w_route, ((0, 0), (0, LANES - w_route.shape[1])))
        whi, wlo = _split_bf16(w_route)
        bias = jnp.concatenate([moe_b_group[i], moe_b_expert[i]])
        bias = jnp.pad(bias, (0, LANES - bias.shape[0])).reshape(1, LANES)
        bucket, rank, counts = _router(h, ffn_norm_w[i], whi, wlo, bias)
        start, tables = _tile_tables(counts, n)
        slot = _slots(bucket, rank, start).reshape(n)
        xs = _dispatch(h, slot, sorted_rows)
        sorted_rows = _moe(xs, ffn_norm_w[i], whi, bias, *expert_w, i, tables)
        ple = (sorted_rows, slot, p_rows, i, ple_norm_w[i], ple_w_gate[i].astype(BF16), ple_w_proj[i].astype(BF16))
        if i == depth - 1:
            h = _ple(*ple, final_norm_w)
        else:
            h, proj = _ple(*ple, attn_norm_w[i + 1], *w_in(i + 1))
    return h.reshape(batch, seq, d)
```
